```python
import math
import jax, jax.numpy as jnp
from jax import lax
import numpy as np

D_MODEL = 2048
BATCH = 1
SEQ = 8192
DEPTH = 4

HEAD_DIM = 64
A_HEADS = 12
A_KV_HEADS = 4
WINDOW = 128
A_BLOCK = 128
B_HEADS = 4
B_DK = 128
B_DV = 128
B_CHUNK = 16
C_HEADS = 12
C_KV_HEADS = 4
C_BLOCK = 128
ROPE_THETA = 10000.0
GRID_W = 64
REL_BUCKETS = 32
REL_MAX_DIST = 128
D_FF = 5632
EPS = 1e-6
NEG = -1e30

A_Q = A_HEADS * HEAD_DIM
A_KV = A_KV_HEADS * HEAD_DIM
B_W = B_HEADS * B_DK
B_VW = B_HEADS * B_DV
C_Q = C_HEADS * HEAD_DIM
C_KV = C_KV_HEADS * HEAD_DIM
IN_SPLITS = [A_Q, A_KV, A_KV, B_W, B_W, B_W, B_VW, B_VW, C_Q, C_KV, C_KV]
D_IN = sum(IN_SPLITS)
D_MIX = A_Q + B_VW + C_Q

kernel_name = "hymba_style_bidir_hybrid_encoder"


def rms_norm(x, w):
    xf = x.astype(jnp.float32)
    y = xf * lax.rsqrt(jnp.mean(xf * xf, axis=-1, keepdims=True) + EPS)
    return (y * w.astype(jnp.float32)).astype(x.dtype)


def swiglu(x, wg, wu, wd):
    return (jax.nn.silu(x @ wg) * (x @ wu)) @ wd


def t5_bucket(rel):
    nb = REL_BUCKETS // 2
    max_exact = nb // 2
    n = jnp.abs(rel)
    nf = jnp.maximum(n, 1).astype(jnp.float32)
    large = max_exact + (jnp.log(nf / max_exact) / math.log(REL_MAX_DIST / max_exact)
                         * (nb - max_exact)).astype(jnp.int32)
    large = jnp.minimum(large, nb - 1)
    return jnp.where(rel > 0, nb, 0) + jnp.where(n < max_exact, n, large)


def axial_rope_tables(L):
    rows = L // GRID_W
    half = HEAD_DIM // 2
    inv = 1.0 / (ROPE_THETA ** (jnp.arange(0, half, 2, dtype=jnp.float32) / half))
    nf = inv.shape[0]
    row_ang = jnp.arange(rows, dtype=jnp.float32)[:, None] * inv
    col_ang = jnp.arange(GRID_W, dtype=jnp.float32)[:, None] * inv
    ang = jnp.concatenate([jnp.broadcast_to(row_ang[:, None, :], (rows, GRID_W, nf)),
                           jnp.broadcast_to(col_ang[None, :, :], (rows, GRID_W, nf))], axis=-1)
    ang = ang.reshape(L, 2 * nf)
    return jnp.cos(ang), jnp.sin(ang)


def apply_rope(x, cos, sin):
    xf = x.astype(jnp.float32).reshape(*x.shape[:-1], HEAD_DIM // 2, 2)
    x0, x1 = xf[..., 0], xf[..., 1]
    c = cos[None, :, None, :]
    s = sin[None, :, None, :]
    out = jnp.stack([x0 * c - x1 * s, x0 * s + x1 * c], axis=-1).reshape(x.shape)
    return out.astype(x.dtype)


def band_windows(t):
    B_, L = t.shape[0], t.shape[1]
    nb = L // A_BLOCK
    tp = jnp.pad(t, ((0, 0), (A_BLOCK, A_BLOCK), (0, 0), (0, 0)))
    tp = tp.reshape(B_, nb + 2, A_BLOCK, *t.shape[2:])
    return jnp.concatenate([tp[:, :-2], tp[:, 1:-1], tp[:, 2:]], axis=2)


def windowed_attention(q, k, v, sink, bias, valid):
    B_, L, _ = q.shape
    nb = L // A_BLOCK
    G = A_HEADS // A_KV_HEADS
    qb = q.reshape(B_, nb, A_BLOCK, A_KV_HEADS, G, HEAD_DIM)
    kw = band_windows(k.reshape(B_, L, A_KV_HEADS, HEAD_DIM))
    vw = band_windows(v.reshape(B_, L, A_KV_HEADS, HEAD_DIM))
    s = jnp.einsum('bnqkgd,bnskd->bnkgqs', qb, kw,
                   preferred_element_type=jnp.float32) * (HEAD_DIM ** -0.5)
    s = jnp.where(valid[None, :, None, None], s + bias, NEG)
    sink_col = jnp.broadcast_to(sink.astype(jnp.float32).reshape(A_KV_HEADS, G, 1, 1),
                                s.shape[:-1] + (1,))
    p = jax.nn.softmax(jnp.concatenate([s, sink_col], axis=-1), axis=-1)[..., :-1]
    o = jnp.einsum('bnkgqs,bnskd->bnqkgd', p.astype(v.dtype), vw)
    return o.reshape(B_, L, A_Q)


def chunked_linear_recurrence(q, k, v, log_f):
    B_, L, H, _ = q.shape
    n = L // B_CHUNK

    def chunks(a):
        return a.astype(jnp.float32).reshape(B_, n, B_CHUNK, H, a.shape[-1]).transpose(1, 0, 3, 2, 4)

    qc, kc, vc, gc = chunks(q), chunks(k), chunks(v), chunks(log_f)
    b = jnp.cumsum(gc, axis=-2)
    tri = jnp.tril(jnp.ones((B_CHUNK, B_CHUNK), dtype=bool))
    diff = jnp.where(tri[:, :, None], b[..., :, None, :] - b[..., None, :, :], -jnp.inf)
    att = jnp.einsum('nbhtc,nbhsc,nbhtsc->nbhts', qc, kc, jnp.exp(diff))
    intra = jnp.einsum('nbhts,nbhsv->nbhtv', att, vc)
    b_last = b[..., -1, :]
    chunk_kv = jnp.einsum('nbhsc,nbhsv->nbhcv', kc * jnp.exp(b_last[..., None, :] - b), vc)

    def step(S, inp):
        decay, kv = inp
        return decay[..., None] * S + kv, S

    S0 = jnp.zeros((B_, H, q.shape[-1], v.shape[-1]), jnp.float32)
    _, S_prev = lax.scan(step, S0, (jnp.exp(b_last), chunk_kv))
    inter = jnp.einsum('nbhtc,nbhcv->nbhtv', qc * jnp.exp(b), S_prev)
    return (intra + inter).transpose(1, 0, 3, 2, 4).reshape(B_, L, H, v.shape[-1])


def hgrn2_bidir(q_raw, zf, zb, i_raw, g_raw, lb, gnorm_w):
    B_, L, _ = q_raw.shape
    q = jax.nn.silu(q_raw).reshape(B_, L, B_HEADS, B_DK)
    v = i_raw.reshape(B_, L, B_HEADS, B_DV)

    def gates(z, lbd):
        z = z.astype(jnp.float32)
        lbd = lbd.astype(jnp.float32)
        log_f = jnp.logaddexp(jnp.log(lbd), jnp.log1p(-lbd) + jax.nn.log_sigmoid(z))
        kk = (1.0 - lbd) * jax.nn.sigmoid(-z)
        return (kk.reshape(B_, L, B_HEADS, B_DK), log_f.reshape(B_, L, B_HEADS, B_DK))

    kf, gf = gates(zf, lb[0])
    kb, gb = gates(zb, lb[1])
    o_fwd = chunked_linear_recurrence(q, kf, v, gf)
    flip = lambda a: jnp.flip(a, axis=1)
    o_bwd = flip(chunked_linear_recurrence(flip(q), flip(kb), flip(v), flip(gb)))
    o = rms_norm(o_fwd + o_bwd, gnorm_w) * jax.nn.silu(
        g_raw.astype(jnp.float32).reshape(B_, L, B_HEADS, B_DV))
    return o.reshape(B_, L, B_VW).astype(q_raw.dtype)


def axial_attention(q, k, v, qk_w, cos, sin):
    B_, L, _ = q.shape
    nb = L // C_BLOCK
    G = C_HEADS // C_KV_HEADS
    q = apply_rope(rms_norm(q.reshape(B_, L, C_HEADS, HEAD_DIM), qk_w[0]), cos, sin)
    k = apply_rope(rms_norm(k.reshape(B_, L, C_KV_HEADS, HEAD_DIM), qk_w[1]), cos, sin)
    v = v.reshape(B_, L, C_KV_HEADS, HEAD_DIM)
    qb = q.reshape(B_, nb, C_BLOCK, C_KV_HEADS, G, HEAD_DIM).transpose(1, 0, 2, 3, 4, 5)

    def block(qblk):
        s = jnp.einsum('bqkgd,bskd->bkgqs', qblk, k,
                       preferred_element_type=jnp.float32) * (HEAD_DIM ** -0.5)
        p = jax.nn.softmax(s, axis=-1)
        return jnp.einsum('bkgqs,bskd->bqkgd', p.astype(v.dtype), v)

    o = lax.map(block, qb)
    return o.transpose(1, 0, 2, 3, 4, 5).reshape(B_, L, C_Q)


def token_mix(h, w_in, w_out, sink, qk_w, lb, gnorm_w, bias, valid, cos, sin):
    proj = h @ w_in
    cuts = [int(c) for c in np.cumsum(IN_SPLITS)[:-1]]
    aq, ak, av, bq, bzf, bzb, bi, bg, cq, ck, cv = jnp.split(proj, cuts, axis=-1)
    ya = windowed_attention(aq, ak, av, sink, bias, valid)
    yb = hgrn2_bidir(bq, bzf, bzb, bi, bg, lb, gnorm_w)
    yc = axial_attention(cq, ck, cv, qk_w, cos, sin)
    return jnp.concatenate([ya, yb, yc], axis=-1) @ w_out


def setup_inputs(seed: int = 0) -> dict:
    key = jax.random.key(seed)
    ks = jax.random.split(key, 16)
    f32 = jnp.float32

    def nrm(k, shape, scale):
        return jax.random.normal(k, shape, f32) * scale

    return {
        "x": nrm(ks[0], (BATCH, SEQ, D_MODEL), 1.0),
        "w_in": nrm(ks[1], (DEPTH, D_MODEL, D_IN), D_MODEL ** -0.5),
        "w_out": nrm(ks[2], (DEPTH, D_MIX, D_MODEL), D_MIX ** -0.5),
        "ffn1_gate": nrm(ks[3], (DEPTH, D_MODEL, D_FF), D_MODEL ** -0.5),
        "ffn1_up": nrm(ks[4], (DEPTH, D_MODEL, D_FF), D_MODEL ** -0.5),
        "ffn1_down": nrm(ks[5], (DEPTH, D_FF, D_MODEL), D_FF ** -0.5),
        "ffn2_gate": nrm(ks[6], (DEPTH, D_MODEL, D_FF), D_MODEL ** -0.5),
        "ffn2_up": nrm(ks[7], (DEPTH, D_MODEL, D_FF), D_MODEL ** -0.5),
        "ffn2_down": nrm(ks[8], (DEPTH, D_FF, D_MODEL), D_FF ** -0.5),
        "norm_w": 1.0 + nrm(ks[9], (DEPTH, 6, D_MODEL), 0.02),
        "sink_logits": nrm(ks[10], (DEPTH, A_HEADS), 0.5),
        "qk_norm_w": 1.0 + nrm(ks[11], (DEPTH, 2, HEAD_DIM), 0.02),
        "hgrn_lb": nrm(ks[12], (DEPTH, 2, B_W), 0.1),
        "hgrn_norm_w": 1.0 + nrm(ks[13], (DEPTH, B_DV), 0.02),
        "rel_bias": nrm(ks[14], (REL_BUCKETS, A_HEADS), 0.1),
    }


def reference(x, w_in, w_out, ffn1_gate, ffn1_up, ffn1_down, ffn2_gate, ffn2_up, ffn2_down,
              norm_w, sink_logits, qk_norm_w, hgrn_lb, hgrn_norm_w, rel_bias):
    B_, L, _ = x.shape
    nb = L // A_BLOCK
    G = A_HEADS // A_KV_HEADS
    qi = jnp.arange(A_BLOCK)[:, None]
    sj = jnp.arange(3 * A_BLOCK)[None, :]
    rel = sj - A_BLOCK - qi
    band = jnp.abs(rel) <= WINDOW
    key_abs = (jnp.arange(nb)[:, None] - 1) * A_BLOCK + jnp.arange(3 * A_BLOCK)[None, :]
    in_range = (key_abs >= 0) & (key_abs < L)
    valid = band[None] & in_range[:, None, :]
    bias = rel_bias.astype(jnp.float32)[t5_bucket(rel)]
    bias = bias.transpose(2, 0, 1).reshape(A_KV_HEADS, G, A_BLOCK, 3 * A_BLOCK)
    cos, sin = axial_rope_tables(L)
    lb_c = jnp.cumsum(jax.nn.softmax(hgrn_lb.astype(jnp.float32), axis=0), axis=0)
    lbs = lb_c - lb_c[0:1]
    for l in range(DEPTH):
        h = rms_norm(x, norm_w[l, 0])
        x = x + 0.5 * rms_norm(swiglu(h, ffn1_gate[l], ffn1_up[l], ffn1_down[l]), norm_w[l, 1])
        h = rms_norm(x, norm_w[l, 2])
        y = token_mix(h, w_in[l], w_out[l], sink_logits[l], qk_norm_w[l], lbs[l],
                      hgrn_norm_w[l], bias, valid, cos, sin)
        x = x + rms_norm(y, norm_w[l, 3])
        h = rms_norm(x, norm_w[l, 4])
        x = x + 0.5 * rms_norm(swiglu(h, ffn2_gate[l], ffn2_up[l], ffn2_down[l]), norm_w[l, 5])
    return x
```

```python
import functools
import math

import jax
import jax.numpy as jnp
import numpy as np
from jax import lax
from jax.experimental import pallas as pl
from jax.experimental.pallas import tpu as pltpu

D_MODEL = 2048
HEAD_DIM = 64
A_HEADS = 12
A_KV_HEADS = 4
WINDOW = 128
A_BLOCK = 128
B_HEADS = 4
B_DK = 128
B_DV = 128
C_HEADS = 12
C_KV_HEADS = 4
ROPE_THETA = 10000.0
GRID_W = 64
REL_BUCKETS = 32
REL_MAX_DIST = 128
D_FF = 5632
EPS = 1e-6
NEG = -1e30

A_Q = A_HEADS * HEAD_DIM
A_KV = A_KV_HEADS * HEAD_DIM
B_W = B_HEADS * B_DK
B_VW = B_HEADS * B_DV
C_Q = C_HEADS * HEAD_DIM
C_KV = C_KV_HEADS * HEAD_DIM

LANES = 128
Q_PAD = A_HEADS * LANES
KV_PAD = A_KV_HEADS * LANES
GROUP = A_HEADS // A_KV_HEADS
VMEM_LIMIT = 52 * 1024 * 1024

HGRN_CHUNK = 128
HGRN_SUB = 16

BF16 = jnp.bfloat16
F32 = jnp.float32


def _params(*sem):
    return pltpu.CompilerParams(dimension_semantics=sem, vmem_limit_bytes=VMEM_LIMIT)


def _dot(a, b):
    return jnp.dot(a, b, preferred_element_type=F32)


def _dot_nt(a, b):
    return lax.dot_general(a, b, (((1,), (1,)), ((), ())), preferred_element_type=F32)


def _rms(x, w):
    ms = jnp.mean(x * x, axis=-1, keepdims=True)
    return x * lax.rsqrt(ms + EPS) * w


def _sigmoid(x):
    return 1.0 / (1.0 + jnp.exp(-x))


def _ffn_body(emit_next, nf, x_ref, prew_ref, postw_ref, nextw_ref, wg_ref, wu_ref, wd_ref,
              *rest):
    if emit_next:
        o_ref, hn_ref, h_scr, acc_scr = rest
    else:
        o_ref, h_scr, acc_scr = rest
    j = pl.program_id(1)

    @pl.when(j == 0)
    def _():
        h_scr[...] = _rms(x_ref[...], prew_ref[...]).astype(BF16)
        acc_scr[...] = jnp.zeros_like(acc_scr)

    h = h_scr[...]
    g = _dot(h, wg_ref[...])
    u = _dot(h, wu_ref[...])
    a = (g * _sigmoid(g)) * u
    acc_scr[...] += _dot(a.astype(BF16), wd_ref[...])

    @pl.when(j == nf - 1)
    def _():
        xn = x_ref[...] + 0.5 * _rms(acc_scr[...], postw_ref[...])
        o_ref[...] = xn
        if emit_next:
            hn_ref[...] = _rms(xn, nextw_ref[...]).astype(BF16)


def _ffn(x, prew, postw, nextw, wg, wu, wd, emit_next):
    L = x.shape[0]
    tm = min(512, L)
    tf = 512
    nf = D_FF // tf
    row = lambda i, j: (i, 0)
    vec = pl.BlockSpec((1, D_MODEL), lambda i, j: (0, 0))
    out_shape = [jax.ShapeDtypeStruct((L, D_MODEL), F32)]
    out_specs = [pl.BlockSpec((tm, D_MODEL), row)]
    if emit_next:
        out_shape.append(jax.ShapeDtypeStruct((L, D_MODEL), BF16))
        out_specs.append(pl.BlockSpec((tm, D_MODEL), row))
    res = pl.pallas_call(
        functools.partial(_ffn_body, emit_next, nf),
        grid=(L // tm, nf),
        in_specs=[pl.BlockSpec((tm, D_MODEL), row), vec, vec, vec,
                  pl.BlockSpec((D_MODEL, tf), lambda i, j: (0, j)),
                  pl.BlockSpec((D_MODEL, tf), lambda i, j: (0, j)),
                  pl.BlockSpec((tf, D_MODEL), lambda i, j: (j, 0))],
        out_specs=out_specs,
        out_shape=out_shape,
        scratch_shapes=[pltpu.VMEM((tm, D_MODEL), BF16), pltpu.VMEM((tm, D_MODEL), F32)],
        compiler_params=_params("parallel", "arbitrary"),
        name="ffn",
    )(x, prew, postw, nextw, wg, wu, wd)
    return res if emit_next else res[0]


def _matmul_body(h_ref, w_ref, o_ref):
    o_ref[...] = _dot(h_ref[...], w_ref[...]).astype(o_ref.dtype)


def _matmul(h, w, out_dtype):
    L, K = h.shape
    N = w.shape[1]
    tm = min(1024, L)
    tn = 512
    return pl.pallas_call(
        _matmul_body,
        grid=(L // tm, N // tn),
        in_specs=[pl.BlockSpec((tm, K), lambda i, j: (i, 0)),
                  pl.BlockSpec((K, tn), lambda i, j: (0, j))],
        out_specs=pl.BlockSpec((tm, tn), lambda i, j: (i, j)),
        out_shape=jax.ShapeDtypeStruct((L, N), out_dtype),
        compiler_params=_params("parallel", "arbitrary"),
        name="in_proj",
    )(h, w)


def _win_attn_body(nb, q_ref, kp_ref, kc_ref, kn_ref, vp_ref, vc_ref, vn_ref, bias_ref,
                   sink_ref, o_ref):
    n = pl.program_id(0)
    qi = lax.broadcasted_iota(jnp.int32, (A_BLOCK, 3 * A_BLOCK), 0)
    sj = lax.broadcasted_iota(jnp.int32, (A_BLOCK, 3 * A_BLOCK), 1)
    rel = sj - A_BLOCK - qi
    key_abs = (n - 1) * A_BLOCK + sj
    valid = (jnp.abs(rel) <= WINDOW) & (key_abs >= 0) & (key_abs < nb * A_BLOCK)
    k_all = jnp.concatenate([kp_ref[...], kc_ref[...], kn_ref[...]], axis=0)
    v_all = jnp.concatenate([vp_ref[...], vc_ref[...], vn_ref[...]], axis=0)
    outs = []
    for h in range(A_HEADS):
        kv = h // GROUP
        q = q_ref[:, h * LANES:(h + 1) * LANES]
        s = _dot_nt(q, k_all[:, kv * LANES:(kv + 1) * LANES])
        s = jnp.where(valid, s + bias_ref[h], NEG)
        sink = sink_ref[h]
        m = jnp.maximum(jnp.max(s, axis=-1, keepdims=True), sink)
        p = jnp.exp(s - m)
        den = jnp.sum(p, axis=-1, keepdims=True) + jnp.exp(sink - m)
        o = _dot(p.astype(BF16), v_all[:, kv * LANES:(kv + 1) * LANES])
        outs.append(o / den)
    for pr in range(A_HEADS // 2):
        packed = outs[2 * pr] + pltpu.roll(outs[2 * pr + 1], HEAD_DIM, axis=1)
        o_ref[:, pr * LANES:(pr + 1) * LANES] = packed.astype(o_ref.dtype)


def _win_attn(proj_a, bias, sink):
    L = proj_a.shape[0]
    nb = L // A_BLOCK
    qcols = Q_PAD // KV_PAD
    prev = lambda n: jnp.maximum(n - 1, 0)
    nxt = lambda n: jnp.minimum(n + 1, nb - 1)
    kblk = lambda f, c: pl.BlockSpec((A_BLOCK, KV_PAD), lambda n: (f(n), c))
    same = lambda n: n
    return pl.pallas_call(
        functools.partial(_win_attn_body, nb),
        grid=(nb,),
        in_specs=[pl.BlockSpec((A_BLOCK, Q_PAD), lambda n: (n, 0)),
                  kblk(prev, qcols), kblk(same, qcols), kblk(nxt, qcols),
                  kblk(prev, qcols + 1), kblk(same, qcols + 1), kblk(nxt, qcols + 1),
                  pl.BlockSpec((A_HEADS, A_BLOCK, 3 * A_BLOCK), lambda n: (0, 0, 0)),
                  pl.BlockSpec((A_HEADS, 1, 1), lambda n: (0, 0, 0))],
        out_specs=pl.BlockSpec((A_BLOCK, A_Q), lambda n: (n, 0)),
        out_shape=jax.ShapeDtypeStruct((L, A_Q), BF16),
        compiler_params=_params("parallel"),
        name="win_attn",
    )(proj_a, proj_a, proj_a, proj_a, proj_a, proj_a, proj_a, bias, sink)


def _cumsum_rows(tri, g):
    g1 = g.astype(BF16)
    r1 = g - g1.astype(F32)
    g2 = r1.astype(BF16)
    g3 = (r1 - g2.astype(F32)).astype(BF16)
    return _dot(tri, g1) + _dot(tri, g2) + _dot(tri, g3)


def _hgrn_chunk(reverse, q, z, v, lb, st_prev):
    C = HGRN_CHUNK
    log_lb = jnp.log(lb)
    log_sig = jnp.minimum(z, 0.0) - jnp.log1p(jnp.exp(-jnp.abs(z)))
    c2 = jnp.log1p(-lb) + log_sig
    g = jnp.maximum(log_lb, c2) + jnp.log1p(jnp.exp(-jnp.abs(log_lb - c2)))
    k = (1.0 - lb) * (1.0 / (1.0 + jnp.exp(z)))

    row = lax.broadcasted_iota(jnp.int32, (C, C), 0)
    col = lax.broadcasted_iota(jnp.int32, (C, C), 1)
    tri = ((col >= row) if reverse else (col <= row)).astype(BF16)
    b = _cumsum_rows(tri, g)
    btot = b[0:1, :] if reverse else b[C - 1:C, :]

    J = C // HGRN_SUB
    b3 = b.reshape(J, HGRN_SUB, LANES)
    q3 = q.reshape(J, HGRN_SUB, LANES)
    k3 = k.reshape(J, HGRN_SUB, LANES)
    v3 = v.reshape(J, HGRN_SUB, LANES)
    tloc = lax.broadcasted_iota(jnp.int32, (J, HGRN_SUB, LANES), 1)
    o3 = jnp.zeros((J, HGRN_SUB, LANES), F32)
    for s in range(HGRN_SUB):
        ok = (tloc <= s) if reverse else (tloc >= s)
        w = jnp.exp(jnp.where(ok, b3 - b3[:, s:s + 1, :], NEG))
        a = jnp.sum(q3 * k3[:, s:s + 1, :] * w, axis=-1, keepdims=True)
        o3 = o3 + a * v3[:, s:s + 1, :]
    o = o3.reshape(C, LANES)

    rowv = lax.broadcasted_iota(jnp.int32, (C, LANES), 0)
    att = jnp.zeros((C, C), F32)
    m = HGRN_SUB
    while m < C:
        bm = b.reshape(C // (2 * m), 2 * m, LANES)
        edge = bm[:, m:m + 1, :] if reverse else bm[:, m - 1:m, :]
        ref = jnp.broadcast_to(edge, bm.shape).reshape(C, LANES)
        odd = ((rowv >> int(math.log2(m))) & 1) == 1
        is_q = jnp.logical_not(odd) if reverse else odd
        x = jnp.exp(jnp.where(is_q, b - ref, ref - b))
        qt = jnp.where(is_q, q * x, 0.0).astype(BF16)
        kt = jnp.where(is_q, 0.0, k * x).astype(BF16)
        am = _dot_nt(qt, kt)
        sh = int(math.log2(2 * m))
        att = att + jnp.where((row >> sh) == (col >> sh), am, 0.0)
        m *= 2
    vb = v.astype(BF16)
    o = o + _dot(att.astype(BF16), vb)

    o = o + _dot_nt((q * jnp.exp(b)).astype(BF16), st_prev.astype(BF16))
    kd = (k * jnp.exp(btot - b)).astype(BF16)
    st_next = jnp.exp(btot) * st_prev + _dot(v.T.astype(BF16), kd)
    return o, st_next


def _hgrn_body(reverse, nch, *refs):
    if reverse:
        q_ref, z_ref, i_ref, lb_ref, of_ref, g_ref, gw_ref, o_ref, s_scr = refs
    else:
        q_ref, z_ref, i_ref, lb_ref, o_ref, s_scr = refs

    @pl.when(pl.program_id(1) == 0)
    def _():
        s_scr[...] = jnp.zeros_like(s_scr)

    lb = lb_ref[0]

    def step(c, carry):
        ci = (nch - 1 - c) if reverse else c
        r0 = pl.multiple_of(ci * HGRN_CHUNK, HGRN_CHUNK)
        rows = pl.ds(r0, HGRN_CHUNK)
        qr = q_ref[rows, :]
        q = qr * _sigmoid(qr)
        o, s_next = _hgrn_chunk(reverse, q, z_ref[rows, :], i_ref[rows, :], lb, s_scr[...])
        s_scr[...] = s_next
        if reverse:
            tot = of_ref[rows, :] + o
            gr = g_ref[rows, :]
            o_ref[rows, :] = (_rms(tot, gw_ref[...]) * (gr * _sigmoid(gr))).astype(o_ref.dtype)
        else:
            o_ref[rows, :] = o
        return carry

    lax.fori_loop(0, nch, step, 0)


def _hgrn(proj_b, lb2, gnorm_w):
    L = proj_b.shape[0]
    tl = min(512, L)
    nt = L // tl
    nch = tl // HGRN_CHUNK
    lb4 = lb2.reshape(2, B_HEADS, 1, B_DK)

    def run(reverse):
        tmap = (lambda t: nt - 1 - t) if reverse else (lambda t: t)
        blk = lambda part: pl.BlockSpec((tl, B_DK), lambda h, t: (tmap(t), part * B_HEADS + h))
        in_specs = [blk(0), blk(2 if reverse else 1), blk(3),
                    pl.BlockSpec((1, 1, B_DK), lambda h, t: (h, 0, 0))]
        args = [proj_b, proj_b, proj_b, lb4[1 if reverse else 0]]
        return in_specs, args, tmap

    in_specs, args, tmap = run(False)
    o_fwd = pl.pallas_call(
        functools.partial(_hgrn_body, False, nch),
        grid=(B_HEADS, nt),
        in_specs=in_specs,
        out_specs=pl.BlockSpec((tl, B_DV), lambda h, t: (t, h)),
        out_shape=jax.ShapeDtypeStruct((L, B_VW), F32),
        scratch_shapes=[pltpu.VMEM((B_DK, B_DV), F32)],
        compiler_params=_params("parallel", "arbitrary"),
        name="hgrn_fwd",
    )(*args)

    in_specs, args, tmap = run(True)
    in_specs += [pl.BlockSpec((tl, B_DV), lambda h, t: (tmap(t), h)),
                 pl.BlockSpec((tl, B_DV), lambda h, t: (tmap(t), 4 * B_HEADS + h)),
                 pl.BlockSpec((1, B_DV), lambda h, t: (0, 0))]
    args += [o_fwd, proj_b, gnorm_w]
    return pl.pallas_call(
        functools.partial(_hgrn_body, True, nch),
        grid=(B_HEADS, nt),
        in_specs=in_specs,
        out_specs=pl.BlockSpec((tl, B_DV), lambda h, t: (tmap(t), h)),
        out_shape=jax.ShapeDtypeStruct((L, B_VW), BF16),
        scratch_shapes=[pltpu.VMEM((B_DK, B_DV), F32)],
        compiler_params=_params("parallel", "arbitrary"),
        name="hgrn_bwd",
    )(*args)


def _rope_prep_body(x_ref, qw_ref, kw_ref, cos_ref, sin_ref, q_ref, kt_ref, v_ref):
    cos = cos_ref[...]
    sin = sin_ref[...]
    lane = lax.broadcasted_iota(jnp.int32, cos.shape, 1)
    even = (lane % 2) == 0

    def norm_rope(x, w):
        ms = jnp.sum(x * x, axis=-1, keepdims=True) * (1.0 / HEAD_DIM)
        y = x * lax.rsqrt(ms + EPS) * w
        partner = jnp.where(even, pltpu.roll(y, LANES - 1, axis=1), pltpu.roll(y, 1, axis=1))
        return y * cos + partner * sin

    for h in range(C_HEADS):
        x = x_ref[:, h * LANES:(h + 1) * LANES]
        q_ref[:, h * LANES:(h + 1) * LANES] = (
            norm_rope(x, qw_ref[...]) * (HEAD_DIM ** -0.5)).astype(q_ref.dtype)
    for h in range(C_KV_HEADS):
        x = x_ref[:, Q_PAD + h * LANES:Q_PAD + (h + 1) * LANES]
        kt_ref[h * LANES:(h + 1) * LANES, :] = norm_rope(x, kw_ref[...]).T.astype(kt_ref.dtype)
    v_ref[...] = x_ref[:, Q_PAD + KV_PAD:].astype(v_ref.dtype)


def _rope_prep(proj_c, qw, kw, cos_t, sin_t):
    L = proj_c.shape[0]
    tl = min(256, L)
    vec = pl.BlockSpec((1, LANES), lambda i: (0, 0))
    tab = pl.BlockSpec((tl, LANES), lambda i: (i, 0))
    return pl.pallas_call(
        _rope_prep_body,
        grid=(L // tl,),
        in_specs=[pl.BlockSpec((tl, Q_PAD + 2 * KV_PAD), lambda i: (i, 0)), vec, vec, tab, tab],
        out_specs=[pl.BlockSpec((tl, Q_PAD), lambda i: (i, 0)),
                   pl.BlockSpec((KV_PAD, tl), lambda i: (0, i)),
                   pl.BlockSpec((tl, KV_PAD), lambda i: (i, 0))],
        out_shape=[jax.ShapeDtypeStruct((L, Q_PAD), BF16),
                   jax.ShapeDtypeStruct((KV_PAD, L), BF16),
                   jax.ShapeDtypeStruct((L, KV_PAD), BF16)],
        compiler_params=_params("parallel"),
        name="rope_prep",
    )(proj_c, qw, kw, cos_t, sin_t)


def _axial_attn_body(tq, tk, nk, q_ref, kt_ref, v_ref, o_ref):
    q = jnp.concatenate([q_ref[:, h * LANES:(h + 1) * LANES] for h in range(GROUP)], axis=0)
    rows = GROUP * tq

    def step(t, carry):
        m, l, acc = carry
        k0 = pl.multiple_of(t * tk, tk)
        s = _dot(q, kt_ref[:, pl.ds(k0, tk)])
        m_new = jnp.maximum(m, jnp.max(s, axis=-1, keepdims=True))
        alpha = jnp.exp(m - m_new)
        p = jnp.exp(s - m_new)
        l = alpha * l + jnp.sum(p, axis=-1, keepdims=True)
        acc = alpha * acc + _dot(p.astype(BF16), v_ref[pl.ds(k0, tk), :])
        return m_new, l, acc

    init = (jnp.full((rows, 1), -jnp.inf, F32), jnp.zeros((rows, 1), F32),
            jnp.zeros((rows, LANES), F32))
    _, l, acc = lax.fori_loop(0, nk, step, init)
    o = acc / l
    o0, o1, o2 = o[0:tq], o[tq:2 * tq], o[2 * tq:3 * tq]
    o_ref[:, 0:LANES] = (o0 + pltpu.roll(o1, HEAD_DIM, axis=1)).astype(o_ref.dtype)
    o_ref[:, LANES:2 * LANES] = o2.astype(o_ref.dtype)


def _axial_attn(q, kt, v):
    L = q.shape[0]
    tq = min(256, L)
    tk = min(512, L)
    return pl.pallas_call(
        functools.partial(_axial_attn_body, tq, tk, L // tk),
        grid=(C_KV_HEADS, L // tq),
        in_specs=[pl.BlockSpec((tq, GROUP * LANES), lambda g, i: (i, g)),
                  pl.BlockSpec((LANES, L), lambda g, i: (g, 0)),
                  pl.BlockSpec((L, LANES), lambda g, i: (0, g))],
        out_specs=pl.BlockSpec((tq, 2 * LANES), lambda g, i: (i, g)),
        out_shape=jax.ShapeDtypeStruct((L, C_KV_HEADS * 2 * LANES), BF16),
        compiler_params=_params("parallel", "arbitrary"),
        name="axial_attn",
    )(q, kt, v)


def _out_proj_body(x_ref, ya_ref, yb_ref, yc_ref, wa_ref, wb_ref, wc_ref, nw_ref, o_ref):
    y = _dot(ya_ref[...], wa_ref[...]) + _dot(yb_ref[...], wb_ref[...]) + _dot(yc_ref[...], wc_ref[...])
    o_ref[...] = x_ref[...] + _rms(y, nw_ref[...])


def _out_proj(x, ya, yb, yc, wa, wb, wc, nw):
    L = x.shape[0]
    tm = min(512, L)
    row = lambda w: pl.BlockSpec((tm, w), lambda i: (i, 0))
    full = lambda a: pl.BlockSpec(a.shape, lambda i: (0, 0))
    return pl.pallas_call(
        _out_proj_body,
        grid=(L // tm,),
        in_specs=[row(D_MODEL), row(ya.shape[1]), row(yb.shape[1]), row(yc.shape[1]),
                  full(wa), full(wb), full(wc), pl.BlockSpec((1, D_MODEL), lambda i: (0, 0))],
        out_specs=row(D_MODEL),
        out_shape=jax.ShapeDtypeStruct((L, D_MODEL), F32),
        compiler_params=_params("parallel"),
        name="out_proj",
    )(x, ya, yb, yc, wa, wb, wc, nw)


def _pad_heads(w, heads):
    w = w.reshape(*w.shape[:-1], heads, HEAD_DIM)
    w = jnp.pad(w, [(0, 0)] * (w.ndim - 1) + [(0, LANES - HEAD_DIM)])
    return w.reshape(*w.shape[:-2], heads * LANES)


def _split_w_in(w_in):
    cuts = np.cumsum([A_Q, A_KV, A_KV, B_W, B_W, B_W, B_VW, B_VW, C_Q, C_KV, C_KV])[:-1]
    aq, ak, av, bq, bzf, bzb, bi, bg, cq, ck, cv = jnp.split(w_in, [int(c) for c in cuts], axis=-1)
    wa = jnp.concatenate([_pad_heads(aq * (HEAD_DIM ** -0.5), A_HEADS),
                          _pad_heads(ak, A_KV_HEADS), _pad_heads(av, A_KV_HEADS)], axis=-1)
    wb = jnp.concatenate([bq, bzf, bzb, bi, bg], axis=-1)
    wc = jnp.concatenate([_pad_heads(cq, C_HEADS), _pad_heads(ck, C_KV_HEADS),
                          _pad_heads(cv, C_KV_HEADS)], axis=-1)
    return wa.astype(BF16), wb.astype(BF16), wc.astype(BF16)


def _split_w_out(w_out):
    wa = w_out[:A_Q]
    wb = w_out[A_Q:A_Q + B_VW]
    wc = w_out[A_Q + B_VW:].reshape(C_KV_HEADS, GROUP * HEAD_DIM, D_MODEL)
    wc = jnp.pad(wc, ((0, 0), (0, 2 * LANES - GROUP * HEAD_DIM), (0, 0)))
    return wa.astype(BF16), wb.astype(BF16), wc.reshape(C_KV_HEADS * 2 * LANES, D_MODEL).astype(BF16)


def _t5_bucket(rel):
    nb = REL_BUCKETS // 2
    max_exact = nb // 2
    n = jnp.abs(rel)
    nf = jnp.maximum(n, 1).astype(F32)
    large = max_exact + (jnp.log(nf / max_exact) / math.log(REL_MAX_DIST / max_exact)
                         * (nb - max_exact)).astype(jnp.int32)
    large = jnp.minimum(large, nb - 1)
    return jnp.where(rel > 0, nb, 0) + jnp.where(n < max_exact, n, large)


def _rope_tables(L):
    half = HEAD_DIM // 2
    inv = 1.0 / (ROPE_THETA ** (jnp.arange(0, half, 2, dtype=F32) / half))
    pos = jnp.arange(L)
    ang = jnp.concatenate([(pos // GRID_W).astype(F32)[:, None] * inv,
                           (pos % GRID_W).astype(F32)[:, None] * inv], axis=-1)
    cos = jnp.repeat(jnp.cos(ang), 2, axis=-1)
    sin = jnp.repeat(jnp.sin(ang), 2, axis=-1) * jnp.tile(jnp.array([-1.0, 1.0], F32), half)
    pad = ((0, 0), (0, LANES - HEAD_DIM))
    return jnp.pad(cos, pad), jnp.pad(sin, pad)


def kernel(x, w_in, w_out, ffn1_gate, ffn1_up, ffn1_down, ffn2_gate, ffn2_up, ffn2_down,
           norm_w, sink_logits, qk_norm_w, hgrn_lb, hgrn_norm_w, rel_bias):
    B_, L, _ = x.shape
    depth = w_in.shape[0]
    qi = jnp.arange(A_BLOCK)[:, None]
    sj = jnp.arange(3 * A_BLOCK)[None, :]
    bias = rel_bias.astype(F32)[_t5_bucket(sj - A_BLOCK - qi)].transpose(2, 0, 1)
    cos_t, sin_t = _rope_tables(L)
    lb_c = jnp.cumsum(jax.nn.softmax(hgrn_lb.astype(F32), axis=0), axis=0)
    lbs = lb_c - lb_c[0:1]
    qkw = jnp.pad(qk_norm_w.astype(F32), ((0, 0), (0, 0), (0, LANES - HEAD_DIM)))
    nw = norm_w.astype(F32).reshape(depth, 6, 1, D_MODEL)

    outs = []
    for b in range(B_):
        xb = x[b]
        for l in range(depth):
            cast = lambda w: w[l].astype(BF16)
            xb, h = _ffn(xb, nw[l, 0], nw[l, 1], nw[l, 2], cast(ffn1_gate), cast(ffn1_up),
                         cast(ffn1_down), True)
            wa, wb, wc = _split_w_in(w_in[l])
            ya = _win_attn(_matmul(h, wa, BF16), bias,
                           sink_logits[l].astype(F32).reshape(A_HEADS, 1, 1))
            yb = _hgrn(_matmul(h, wb, F32), lbs[l], hgrn_norm_w[l].astype(F32).reshape(1, B_DV))
            q, kt, v = _rope_prep(_matmul(h, wc, F32), qkw[l, 0:1], qkw[l, 1:2], cos_t, sin_t)
            yc = _axial_attn(q, kt, v)
            oa, ob, oc = _split_w_out(w_out[l])
            xb = _out_proj(xb, ya, yb, yc, oa, ob, oc, nw[l, 3])
            xb = _ffn(xb, nw[l, 4], nw[l, 5], nw[l, 5], cast(ffn2_gate), cast(ffn2_up),
                      cast(ffn2_down), False)
        outs.append(xb)
    return jnp.stack(outs, axis=0)
```

```python
import functools
import math

import jax
import jax.numpy as jnp
import numpy as np
from jax import lax
from jax.experimental import pallas as pl
from jax.experimental.pallas import tpu as pltpu

D_MODEL = 2048
HEAD_DIM = 64
A_HEADS = 12
A_KV_HEADS = 4
WINDOW = 128
A_BLOCK = 128
B_HEADS = 4
B_DK = 128
B_DV = 128
C_HEADS = 12
C_KV_HEADS = 4
ROPE_THETA = 10000.0
GRID_W = 64
REL_BUCKETS = 32
REL_MAX_DIST = 128
D_FF = 5632
EPS = 1e-6
NEG = -1e30

A_Q = A_HEADS * HEAD_DIM
A_KV = A_KV_HEADS * HEAD_DIM
B_W = B_HEADS * B_DK
B_VW = B_HEADS * B_DV
C_Q = C_HEADS * HEAD_DIM
C_KV = C_KV_HEADS * HEAD_DIM

LANES = 128
Q_PAD = A_HEADS * LANES
KV_PAD = A_KV_HEADS * LANES
GROUP = A_HEADS // A_KV_HEADS
VMEM_LIMIT = 52 * 1024 * 1024

LOG2E = math.log2(math.e)
AXIAL_UNROLL = 4

HGRN_CHUNK = 128

BF16 = jnp.bfloat16
F32 = jnp.float32


def _params(*sem):
    return pltpu.CompilerParams(dimension_semantics=sem, vmem_limit_bytes=VMEM_LIMIT)


def _dot(a, b):
    return jnp.dot(a, b, preferred_element_type=F32)


def _dot_nt(a, b):
    return lax.dot_general(a, b, (((1,), (1,)), ((), ())), preferred_element_type=F32)


def _rms(x, w):
    ms = jnp.mean(x * x, axis=-1, keepdims=True)
    return x * lax.rsqrt(ms + EPS) * w


def _sigmoid(x):
    return 1.0 / (1.0 + jnp.exp(-x))


def _ffn_body(emit_next, nf, x_ref, prew_ref, postw_ref, nextw_ref, wg_ref, wu_ref, wd_ref,
              *rest):
    if emit_next:
        o_ref, hn_ref, h_scr, acc_scr = rest
    else:
        o_ref, h_scr, acc_scr = rest
    j = pl.program_id(1)

    @pl.when(j == 0)
    def _():
        h_scr[...] = _rms(x_ref[...], prew_ref[...]).astype(BF16)
        acc_scr[...] = jnp.zeros_like(acc_scr)

    h = h_scr[...]
    g = _dot(h, wg_ref[...])
    u = _dot(h, wu_ref[...])
    a = (g * _sigmoid(g)) * u
    acc_scr[...] += _dot(a.astype(BF16), wd_ref[...])

    @pl.when(j == nf - 1)
    def _():
        xn = x_ref[...] + 0.5 * _rms(acc_scr[...], postw_ref[...])
        o_ref[...] = xn
        if emit_next:
            hn_ref[...] = _rms(xn, nextw_ref[...]).astype(BF16)


def _ffn(x, prew, postw, nextw, wg, wu, wd, emit_next):
    L = x.shape[0]
    tm = min(512, L)
    tf = 512
    nf = D_FF // tf
    row = lambda i, j: (i, 0)
    vec = pl.BlockSpec((1, D_MODEL), lambda i, j: (0, 0))
    out_shape = [jax.ShapeDtypeStruct((L, D_MODEL), F32)]
    out_specs = [pl.BlockSpec((tm, D_MODEL), row)]
    if emit_next:
        out_shape.append(jax.ShapeDtypeStruct((L, D_MODEL), BF16))
        out_specs.append(pl.BlockSpec((tm, D_MODEL), row))
    res = pl.pallas_call(
        functools.partial(_ffn_body, emit_next, nf),
        grid=(L // tm, nf),
        in_specs=[pl.BlockSpec((tm, D_MODEL), row), vec, vec, vec,
                  pl.BlockSpec((D_MODEL, tf), lambda i, j: (0, j)),
                  pl.BlockSpec((D_MODEL, tf), lambda i, j: (0, j)),
                  pl.BlockSpec((tf, D_MODEL), lambda i, j: (j, 0))],
        out_specs=out_specs,
        out_shape=out_shape,
        scratch_shapes=[pltpu.VMEM((tm, D_MODEL), BF16), pltpu.VMEM((tm, D_MODEL), F32)],
        compiler_params=_params("parallel", "arbitrary"),
        name="ffn",
    )(x, prew, postw, nextw, wg, wu, wd)
    return res if emit_next else res[0]


def _matmul_body(h_ref, w_ref, o_ref):
    o_ref[...] = _dot(h_ref[...], w_ref[...]).astype(o_ref.dtype)


def _matmul(h, w, out_dtype):
    L, K = h.shape
    N = w.shape[1]
    tm = min(1024, L)
    tn = 512
    return pl.pallas_call(
        _matmul_body,
        grid=(L // tm, N // tn),
        in_specs=[pl.BlockSpec((tm, K), lambda i, j: (i, 0)),
                  pl.BlockSpec((K, tn), lambda i, j: (0, j))],
        out_specs=pl.BlockSpec((tm, tn), lambda i, j: (i, j)),
        out_shape=jax.ShapeDtypeStruct((L, N), out_dtype),
        compiler_params=_params("parallel", "arbitrary"),
        name="in_proj",
    )(h, w)


def _win_attn_body(nb, q_ref, kp_ref, kc_ref, kn_ref, vp_ref, vc_ref, vn_ref, bias_ref,
                   sink_ref, o_ref):
    n = pl.program_id(0)
    edge_lo = jnp.where(n == 0, NEG, 0.0)
    edge_hi = jnp.where(n == nb - 1, NEG, 0.0)
    k_all = jnp.concatenate([kp_ref[...], kc_ref[...], kn_ref[...]], axis=0)
    v_all = jnp.concatenate([vp_ref[...], vc_ref[...], vn_ref[...]], axis=0)
    outs = []
    for g in range(A_KV_HEADS):
        heads = range(g * GROUP, (g + 1) * GROUP)
        q3 = jnp.concatenate([q_ref[:, h * LANES:(h + 1) * LANES] for h in heads], axis=0)
        s3 = _dot_nt(q3, k_all[:, g * LANES:(g + 1) * LANES])
        ps, dens = [], []
        for j, h in enumerate(heads):
            s = s3[j * A_BLOCK:(j + 1) * A_BLOCK] + bias_ref[h]
            t = [s[:, 0:LANES] + edge_lo, s[:, LANES:2 * LANES], s[:, 2 * LANES:] + edge_hi]
            sink = sink_ref[h]
            m = jnp.maximum(jnp.max(jnp.maximum(jnp.maximum(t[0], t[1]), t[2]), axis=-1,
                                    keepdims=True), sink)
            p = [jnp.exp2(ti - m) for ti in t]
            dens.append(jnp.sum(p[0] + p[1] + p[2], axis=-1, keepdims=True) + jnp.exp2(sink - m))
            ps.append(jnp.concatenate(p, axis=1).astype(BF16))
        o3 = _dot(jnp.concatenate(ps, axis=0), v_all[:, g * LANES:(g + 1) * LANES])
        for j in range(GROUP):
            outs.append(o3[j * A_BLOCK:(j + 1) * A_BLOCK] / dens[j])
    for pr in range(A_HEADS // 2):
        packed = outs[2 * pr] + pltpu.roll(outs[2 * pr + 1], HEAD_DIM, axis=1)
        o_ref[:, pr * LANES:(pr + 1) * LANES] = packed.astype(o_ref.dtype)


def _win_attn(proj_a, bias, sink):
    L = proj_a.shape[0]
    nb = L // A_BLOCK
    qcols = Q_PAD // KV_PAD
    prev = lambda n: jnp.maximum(n - 1, 0)
    nxt = lambda n: jnp.minimum(n + 1, nb - 1)
    kblk = lambda f, c: pl.BlockSpec((A_BLOCK, KV_PAD), lambda n: (f(n), c))
    same = lambda n: n
    return pl.pallas_call(
        functools.partial(_win_attn_body, nb),
        grid=(nb,),
        in_specs=[pl.BlockSpec((A_BLOCK, Q_PAD), lambda n: (n, 0)),
                  kblk(prev, qcols), kblk(same, qcols), kblk(nxt, qcols),
                  kblk(prev, qcols + 1), kblk(same, qcols + 1), kblk(nxt, qcols + 1),
                  pl.BlockSpec((A_HEADS, A_BLOCK, 3 * A_BLOCK), lambda n: (0, 0, 0)),
                  pl.BlockSpec((A_HEADS, 1, LANES), lambda n: (0, 0, 0))],
        out_specs=pl.BlockSpec((A_BLOCK, A_Q), lambda n: (n, 0)),
        out_shape=jax.ShapeDtypeStruct((L, A_Q), BF16),
        compiler_params=_params("parallel"),
        name="win_attn",
    )(proj_a, proj_a, proj_a, proj_a, proj_a, proj_a, proj_a, bias, sink)


def _cumsum_rows(tri, g):
    g1 = g.astype(BF16)
    r1 = g - g1.astype(F32)
    g2 = r1.astype(BF16)
    g3 = (r1 - g2.astype(F32)).astype(BF16)
    return _dot(tri, g1) + _dot(tri, g2) + _dot(tri, g3)


def _hgrn_tables(reverse):
    C = HGRN_CHUNK
    row = lax.broadcasted_iota(jnp.int32, (C, C), 0)
    col = lax.broadcasted_iota(jnp.int32, (C, C), 1)
    tri = ((col >= row) if reverse else (col <= row)).astype(BF16)
    diff = (row ^ col).astype(F32)
    high = (lax.bitcast_convert_type(diff, jnp.int32) >> 23) - 127
    wrong_side = (row < col) if not reverse else (row > col)
    level = jnp.where(row == col, 0, jnp.where(wrong_side, -1, high + 1))
    return tri, level


def _hgrn_chunk(reverse, q, z, v, lb, st_prev, tri, level):
    C = HGRN_CHUNK
    log_lb = jnp.log(lb)
    log_sig = jnp.minimum(z, 0.0) - jnp.log1p(jnp.exp(-jnp.abs(z)))
    c2 = jnp.log1p(-lb) + log_sig
    g = jnp.maximum(log_lb, c2) + jnp.log1p(jnp.exp(-jnp.abs(log_lb - c2)))
    k = (1.0 - lb) * (1.0 / (1.0 + jnp.exp(z)))

    b = _cumsum_rows(tri, g)
    btot = b[0:1, :] if reverse else b[C - 1:C, :]

    rowv = lax.broadcasted_iota(jnp.int32, (C, LANES), 0)
    att = jnp.where(level == 0, _dot_nt(q.astype(BF16), k.astype(BF16)), 0.0)
    seg = b
    for j in range(int(math.log2(C))):
        m = 1 << j
        later = ((rowv >> j) & 1) == 1
        up = pltpu.roll(seg, m, axis=0)
        down = pltpu.roll(seg, C - m, axis=0)
        if reverse:
            ref = jnp.where(later, seg, down)
            seg = jnp.where(later, up, seg)
        else:
            ref = jnp.where(later, up, seg)
            seg = jnp.where(later, seg, down)
        x = jnp.exp(-jnp.abs(b - ref))
        am = _dot_nt((q * x).astype(BF16), (k * x).astype(BF16))
        att = att + jnp.where(level == j + 1, am, 0.0)
    vb = v.astype(BF16)
    o = _dot(att.astype(BF16), vb)

    o = o + _dot_nt((q * jnp.exp(b)).astype(BF16), st_prev.astype(BF16))
    kd = (k * jnp.exp(btot - b)).astype(BF16)
    st_next = jnp.exp(btot) * st_prev + _dot(v.T.astype(BF16), kd)
    return o, st_next


def _hgrn_body(reverse, nch, *refs):
    if reverse:
        q_ref, z_ref, i_ref, lb_ref, of_ref, g_ref, gw_ref, o_ref, s_scr = refs
    else:
        q_ref, z_ref, i_ref, lb_ref, o_ref, s_scr = refs

    @pl.when(pl.program_id(1) == 0)
    def _():
        s_scr[...] = jnp.zeros_like(s_scr)

    lb = lb_ref[0]
    tri, level = _hgrn_tables(reverse)

    def step(c, carry):
        ci = (nch - 1 - c) if reverse else c
        r0 = pl.multiple_of(ci * HGRN_CHUNK, HGRN_CHUNK)
        rows = pl.ds(r0, HGRN_CHUNK)
        qr = q_ref[rows, :]
        q = qr * _sigmoid(qr)
        o, s_next = _hgrn_chunk(reverse, q, z_ref[rows, :], i_ref[rows, :], lb, s_scr[...],
                                tri, level)
        s_scr[...] = s_next
        if reverse:
            tot = of_ref[rows, :] + o
            gr = g_ref[rows, :]
            o_ref[rows, :] = (_rms(tot, gw_ref[...]) * (gr * _sigmoid(gr))).astype(o_ref.dtype)
        else:
            o_ref[rows, :] = o
        return carry

    lax.fori_loop(0, nch, step, 0)


def _hgrn(proj_b, lb2, gnorm_w):
    L = proj_b.shape[0]
    tl = min(512, L)
    nt = L // tl
    nch = tl // HGRN_CHUNK
    lb4 = lb2.reshape(2, B_HEADS, 1, B_DK)

    def run(reverse):
        tmap = (lambda t: nt - 1 - t) if reverse else (lambda t: t)
        blk = lambda part: pl.BlockSpec((tl, B_DK), lambda h, t: (tmap(t), part * B_HEADS + h))
        in_specs = [blk(0), blk(2 if reverse else 1), blk(3),
                    pl.BlockSpec((1, 1, B_DK), lambda h, t: (h, 0, 0))]
        args = [proj_b, proj_b, proj_b, lb4[1 if reverse else 0]]
        return in_specs, args, tmap

    in_specs, args, tmap = run(False)
    o_fwd = pl.pallas_call(
        functools.partial(_hgrn_body, False, nch),
        grid=(B_HEADS, nt),
        in_specs=in_specs,
        out_specs=pl.BlockSpec((tl, B_DV), lambda h, t: (t, h)),
        out_shape=jax.ShapeDtypeStruct((L, B_VW), F32),
        scratch_shapes=[pltpu.VMEM((B_DK, B_DV), F32)],
        compiler_params=_params("parallel", "arbitrary"),
        name="hgrn_fwd",
    )(*args)

    in_specs, args, tmap = run(True)
    in_specs += [pl.BlockSpec((tl, B_DV), lambda h, t: (tmap(t), h)),
                 pl.BlockSpec((tl, B_DV), lambda h, t: (tmap(t), 4 * B_HEADS + h)),
                 pl.BlockSpec((1, B_DV), lambda h, t: (0, 0))]
    args += [o_fwd, proj_b, gnorm_w]
    return pl.pallas_call(
        functools.partial(_hgrn_body, True, nch),
        grid=(B_HEADS, nt),
        in_specs=in_specs,
        out_specs=pl.BlockSpec((tl, B_DV), lambda h, t: (tmap(t), h)),
        out_shape=jax.ShapeDtypeStruct((L, B_VW), BF16),
        scratch_shapes=[pltpu.VMEM((B_DK, B_DV), F32)],
        compiler_params=_params("parallel", "arbitrary"),
        name="hgrn_bwd",
    )(*args)


def _rope_prep_body(x_ref, qw_ref, kw_ref, cos_ref, sin_ref, q_ref, kt_ref, v_ref):
    cos = cos_ref[...]
    sin = sin_ref[...]
    lane = lax.broadcasted_iota(jnp.int32, cos.shape, 1)
    even = (lane % 2) == 0

    def norm_rope(x, w):
        ms = jnp.sum(x * x, axis=-1, keepdims=True) * (1.0 / HEAD_DIM)
        y = x * lax.rsqrt(ms + EPS) * w
        partner = jnp.where(even, pltpu.roll(y, LANES - 1, axis=1), pltpu.roll(y, 1, axis=1))
        return y * cos + partner * sin

    for h in range(C_HEADS):
        x = x_ref[:, h * LANES:(h + 1) * LANES]
        q_ref[:, h * LANES:(h + 1) * LANES] = (
            norm_rope(x, qw_ref[...]) * (LOG2E * HEAD_DIM ** -0.5)).astype(q_ref.dtype)
    for h in range(C_KV_HEADS):
        x = x_ref[:, Q_PAD + h * LANES:Q_PAD + (h + 1) * LANES]
        kt_ref[h * LANES:(h + 1) * LANES, :] = norm_rope(x, kw_ref[...]).T.astype(kt_ref.dtype)
    v = x_ref[:, Q_PAD + KV_PAD:]
    vlane = lax.broadcasted_iota(jnp.int32, v.shape, 1)
    v_ref[...] = jnp.where(vlane % LANES == HEAD_DIM, 1.0, v).astype(v_ref.dtype)


def _rope_prep(proj_c, qw, kw, cos_t, sin_t):
    L = proj_c.shape[0]
    tl = min(256, L)
    vec = pl.BlockSpec((1, LANES), lambda i: (0, 0))
    tab = pl.BlockSpec((tl, LANES), lambda i: (i, 0))
    return pl.pallas_call(
        _rope_prep_body,
        grid=(L // tl,),
        in_specs=[pl.BlockSpec((tl, Q_PAD + 2 * KV_PAD), lambda i: (i, 0)), vec, vec, tab, tab],
        out_specs=[pl.BlockSpec((tl, Q_PAD), lambda i: (i, 0)),
                   pl.BlockSpec((KV_PAD, tl), lambda i: (0, i)),
                   pl.BlockSpec((tl, KV_PAD), lambda i: (i, 0))],
        out_shape=[jax.ShapeDtypeStruct((L, Q_PAD), BF16),
                   jax.ShapeDtypeStruct((KV_PAD, L), BF16),
                   jax.ShapeDtypeStruct((L, KV_PAD), BF16)],
        compiler_params=_params("parallel"),
        name="rope_prep",
    )(proj_c, qw, kw, cos_t, sin_t)


def _axial_attn_body(tq, tk, nk, q_ref, kt_ref, v_ref, o_ref, m_scr, acc_scr):
    q = jnp.concatenate([q_ref[:, h * LANES:(h + 1) * LANES] for h in range(GROUP)], axis=0)
    m_scr[...] = jnp.full(m_scr.shape, -jnp.inf, F32)
    acc_scr[...] = jnp.zeros(acc_scr.shape, F32)

    def step(t, carry):
        k0 = pl.multiple_of(t * tk, tk)
        s = _dot(q, kt_ref[:, pl.ds(k0, tk)])
        parts = [s[:, i * LANES:(i + 1) * LANES] for i in range(tk // LANES)]
        m = m_scr[...]
        m_new = jnp.maximum(m, jnp.max(functools.reduce(jnp.maximum, parts), axis=-1,
                                       keepdims=True))
        p = jnp.concatenate([jnp.exp2(part - m_new) for part in parts], axis=1)
        acc_scr[...] = jnp.exp2(m - m_new) * acc_scr[...] + _dot(p.astype(BF16),
                                                                  v_ref[pl.ds(k0, tk), :])
        m_scr[...] = m_new
        return carry

    lax.fori_loop(0, nk, step, 0, unroll=AXIAL_UNROLL)
    acc = acc_scr[...]
    lane = lax.broadcasted_iota(jnp.int32, acc.shape, 1)
    o = jnp.where(lane < HEAD_DIM, acc / acc[:, HEAD_DIM:HEAD_DIM + 1], 0.0)
    o0, o1, o2 = o[0:tq], o[tq:2 * tq], o[2 * tq:3 * tq]
    o_ref[:, 0:LANES] = (o0 + pltpu.roll(o1, HEAD_DIM, axis=1)).astype(o_ref.dtype)
    o_ref[:, LANES:2 * LANES] = o2.astype(o_ref.dtype)


def _axial_attn(q, kt, v):
    L = q.shape[0]
    tq = min(256, L)
    tk = min(512, L)
    return pl.pallas_call(
        functools.partial(_axial_attn_body, tq, tk, L // tk),
        grid=(C_KV_HEADS, L // tq),
        in_specs=[pl.BlockSpec((tq, GROUP * LANES), lambda g, i: (i, g)),
                  pl.BlockSpec((LANES, L), lambda g, i: (g, 0)),
                  pl.BlockSpec((L, LANES), lambda g, i: (0, g))],
        out_specs=pl.BlockSpec((tq, 2 * LANES), lambda g, i: (i, g)),
        out_shape=jax.ShapeDtypeStruct((L, C_KV_HEADS * 2 * LANES), BF16),
        scratch_shapes=[pltpu.VMEM((GROUP * tq, LANES), F32), pltpu.VMEM((GROUP * tq, LANES), F32)],
        compiler_params=_params("parallel", "arbitrary"),
        name="axial_attn",
    )(q, kt, v)


def _out_proj_body(x_ref, ya_ref, yb_ref, yc_ref, wa_ref, wb_ref, wc_ref, nw_ref, o_ref):
    y = _dot(ya_ref[...], wa_ref[...]) + _dot(yb_ref[...], wb_ref[...]) + _dot(yc_ref[...], wc_ref[...])
    o_ref[...] = x_ref[...] + _rms(y, nw_ref[...])


def _out_proj(x, ya, yb, yc, wa, wb, wc, nw):
    L = x.shape[0]
    tm = min(512, L)
    row = lambda w: pl.BlockSpec((tm, w), lambda i: (i, 0))
    full = lambda a: pl.BlockSpec(a.shape, lambda i: (0, 0))
    return pl.pallas_call(
        _out_proj_body,
        grid=(L // tm,),
        in_specs=[row(D_MODEL), row(ya.shape[1]), row(yb.shape[1]), row(yc.shape[1]),
                  full(wa), full(wb), full(wc), pl.BlockSpec((1, D_MODEL), lambda i: (0, 0))],
        out_specs=row(D_MODEL),
        out_shape=jax.ShapeDtypeStruct((L, D_MODEL), F32),
        compiler_params=_params("parallel"),
        name="out_proj",
    )(x, ya, yb, yc, wa, wb, wc, nw)


def _pad_heads(w, heads):
    w = w.reshape(*w.shape[:-1], heads, HEAD_DIM)
    w = jnp.pad(w, [(0, 0)] * (w.ndim - 1) + [(0, LANES - HEAD_DIM)])
    return w.reshape(*w.shape[:-2], heads * LANES)


def _split_w_in(w_in):
    cuts = np.cumsum([A_Q, A_KV, A_KV, B_W, B_W, B_W, B_VW, B_VW, C_Q, C_KV, C_KV])[:-1]
    aq, ak, av, bq, bzf, bzb, bi, bg, cq, ck, cv = jnp.split(w_in, [int(c) for c in cuts], axis=-1)
    wa = jnp.concatenate([_pad_heads(aq * (LOG2E * HEAD_DIM ** -0.5), A_HEADS),
                          _pad_heads(ak, A_KV_HEADS), _pad_heads(av, A_KV_HEADS)], axis=-1)
    wb = jnp.concatenate([bq, bzf, bzb, bi, bg], axis=-1)
    wc = jnp.concatenate([_pad_heads(cq, C_HEADS), _pad_heads(ck, C_KV_HEADS),
                          _pad_heads(cv, C_KV_HEADS)], axis=-1)
    return wa.astype(BF16), wb.astype(BF16), wc.astype(BF16)


def _split_w_out(w_out):
    wa = w_out[:A_Q]
    wb = w_out[A_Q:A_Q + B_VW]
    wc = w_out[A_Q + B_VW:].reshape(C_KV_HEADS, GROUP * HEAD_DIM, D_MODEL)
    wc = jnp.pad(wc, ((0, 0), (0, 2 * LANES - GROUP * HEAD_DIM), (0, 0)))
    return wa.astype(BF16), wb.astype(BF16), wc.reshape(C_KV_HEADS * 2 * LANES, D_MODEL).astype(BF16)


def _t5_bucket(rel):
    nb = REL_BUCKETS // 2
    max_exact = nb // 2
    n = jnp.abs(rel)
    nf = jnp.maximum(n, 1).astype(F32)
    large = max_exact + (jnp.log(nf / max_exact) / math.log(REL_MAX_DIST / max_exact)
                         * (nb - max_exact)).astype(jnp.int32)
    large = jnp.minimum(large, nb - 1)
    return jnp.where(rel > 0, nb, 0) + jnp.where(n < max_exact, n, large)


def _rope_tables(L):
    half = HEAD_DIM // 2
    inv = 1.0 / (ROPE_THETA ** (jnp.arange(0, half, 2, dtype=F32) / half))
    pos = jnp.arange(L)
    ang = jnp.concatenate([(pos // GRID_W).astype(F32)[:, None] * inv,
                           (pos % GRID_W).astype(F32)[:, None] * inv], axis=-1)
    cos = jnp.repeat(jnp.cos(ang), 2, axis=-1)
    sin = jnp.repeat(jnp.sin(ang), 2, axis=-1) * jnp.tile(jnp.array([-1.0, 1.0], F32), half)
    pad = ((0, 0), (0, LANES - HEAD_DIM))
    return jnp.pad(cos, pad), jnp.pad(sin, pad)


def kernel(x, w_in, w_out, ffn1_gate, ffn1_up, ffn1_down, ffn2_gate, ffn2_up, ffn2_down,
           norm_w, sink_logits, qk_norm_w, hgrn_lb, hgrn_norm_w, rel_bias):
    B_, L, _ = x.shape
    depth = w_in.shape[0]
    qi = jnp.arange(A_BLOCK)[:, None]
    sj = jnp.arange(3 * A_BLOCK)[None, :]
    rel = sj - A_BLOCK - qi
    bias = rel_bias.astype(F32)[_t5_bucket(rel)].transpose(2, 0, 1)
    bias = jnp.where(jnp.abs(rel) <= WINDOW, bias * LOG2E, NEG)
    cos_t, sin_t = _rope_tables(L)
    lb_c = jnp.cumsum(jax.nn.softmax(hgrn_lb.astype(F32), axis=0), axis=0)
    lbs = lb_c - lb_c[0:1]
    qkw = jnp.pad(qk_norm_w.astype(F32), ((0, 0), (0, 0), (0, LANES - HEAD_DIM)))
    nw = norm_w.astype(F32).reshape(depth, 6, 1, D_MODEL)

    outs = []
    for b in range(B_):
        xb = x[b]
        for l in range(depth):
            cast = lambda w: w[l].astype(BF16)
            xb, h = _ffn(xb, nw[l, 0], nw[l, 1], nw[l, 2], cast(ffn1_gate), cast(ffn1_up),
                         cast(ffn1_down), True)
            wa, wb, wc = _split_w_in(w_in[l])
            ya = _win_attn(_matmul(h, wa, BF16), bias,
                           jnp.broadcast_to((sink_logits[l].astype(F32) * LOG2E)
                                            .reshape(A_HEADS, 1, 1), (A_HEADS, 1, LANES)))
            yb = _hgrn(_matmul(h, wb, F32), lbs[l], hgrn_norm_w[l].astype(F32).reshape(1, B_DV))
            q, kt, v = _rope_prep(_matmul(h, wc, F32), qkw[l, 0:1], qkw[l, 1:2], cos_t, sin_t)
            yc = _axial_attn(q, kt, v)
            oa, ob, oc = _split_w_out(w_out[l])
            xb = _out_proj(xb, ya, yb, yc, oa, ob, oc, nw[l, 3])
            xb = _ffn(xb, nw[l, 4], nw[l, 5], nw[l, 5], cast(ffn2_gate), cast(ffn2_up),
                      cast(ffn2_down), False)
        outs.append(xb)
    return jnp.stack(outs, axis=0)
```

```python
import functools
import math

import jax
import jax.numpy as jnp
from jax import lax
from jax.experimental import pallas as pl
from jax.experimental.pallas import tpu as pltpu

D_MODEL = 2048
HEAD_DIM = 64
A_HEADS = 12
A_KV_HEADS = 4
WINDOW = 128
A_BLOCK = 128
B_HEADS = 4
B_DK = 128
B_DV = 128
C_HEADS = 12
C_KV_HEADS = 4
ROPE_THETA = 10000.0
GRID_W = 64
REL_BUCKETS = 32
REL_MAX_DIST = 128
D_FF = 5632
EPS = 1e-6
NEG = -1e30

A_Q = A_HEADS * HEAD_DIM
A_KV = A_KV_HEADS * HEAD_DIM
B_W = B_HEADS * B_DK
B_VW = B_HEADS * B_DV
C_Q = C_HEADS * HEAD_DIM
C_KV = C_KV_HEADS * HEAD_DIM
B_COL0 = A_Q + 2 * A_KV

LANES = 128
SUBLANES = 8
Q_PAD = A_HEADS * LANES
KV_PAD = A_KV_HEADS * LANES
GROUP = A_HEADS // A_KV_HEADS
VMEM_LIMIT = 52 * 1024 * 1024

LOG2E = math.log2(math.e)
AXIAL_UNROLL = 8

HGRN_CHUNK = 128

BF16 = jnp.bfloat16
F32 = jnp.float32


def _params(*sem):
    return pltpu.CompilerParams(dimension_semantics=sem, vmem_limit_bytes=VMEM_LIMIT)


def _dot(a, b):
    return jnp.dot(a, b, preferred_element_type=F32)


def _dot_nt(a, b):
    return lax.dot_general(a, b, (((1,), (1,)), ((), ())), preferred_element_type=F32)


def _rms(x, w):
    ms = jnp.mean(x * x, axis=-1, keepdims=True)
    return x * lax.rsqrt(ms + EPS) * w


def _sigmoid(x):
    return 1.0 / (1.0 + jnp.exp(-x))


def _ffn_body(emit_next, nf, x_ref, prew_ref, postw_ref, *rest):
    if emit_next:
        nextw_ref, wg_ref, wu_ref, wd_ref, o_ref, hn_ref, h_scr, acc_scr = rest
    else:
        wg_ref, wu_ref, wd_ref, o_ref, h_scr, acc_scr = rest
    j = pl.program_id(1)

    @pl.when(j == 0)
    def _():
        h_scr[...] = _rms(x_ref[...], prew_ref[...]).astype(BF16)
        acc_scr[...] = jnp.zeros_like(acc_scr)

    h = h_scr[...]
    g = _dot(h, wg_ref[...])
    u = _dot(h, wu_ref[...])
    a = (g * _sigmoid(g)) * u
    acc_scr[...] += _dot(a.astype(BF16), wd_ref[...])

    @pl.when(j == nf - 1)
    def _():
        xn = x_ref[...] + 0.5 * _rms(acc_scr[...], postw_ref[...])
        o_ref[...] = xn
        if emit_next:
            hn_ref[...] = _rms(xn, nextw_ref[...]).astype(BF16)


def _ffn(x, prew, postw, wg, wu, wd, nextw=None):
    emit_next = nextw is not None
    L = x.shape[0]
    tm = min(512, L)
    tf = 512
    nf = D_FF // tf
    row = lambda i, j: (i, 0)
    vec = pl.BlockSpec((1, D_MODEL), lambda i, j: (0, 0))
    out_shape = [jax.ShapeDtypeStruct((L, D_MODEL), F32)]
    out_specs = [pl.BlockSpec((tm, D_MODEL), row)]
    if emit_next:
        out_shape.append(jax.ShapeDtypeStruct((L, D_MODEL), BF16))
        out_specs.append(pl.BlockSpec((tm, D_MODEL), row))
    norm_args = [prew, postw] + ([nextw] if emit_next else [])
    res = pl.pallas_call(
        functools.partial(_ffn_body, emit_next, nf),
        grid=(L // tm, nf),
        in_specs=[pl.BlockSpec((tm, D_MODEL), row)] + [vec] * len(norm_args) + [
            pl.BlockSpec((D_MODEL, tf), lambda i, j: (0, j)),
            pl.BlockSpec((D_MODEL, tf), lambda i, j: (0, j)),
            pl.BlockSpec((tf, D_MODEL), lambda i, j: (j, 0))],
        out_specs=out_specs,
        out_shape=out_shape,
        scratch_shapes=[pltpu.VMEM((tm, D_MODEL), BF16), pltpu.VMEM((tm, D_MODEL), F32)],
        compiler_params=_params("parallel", "arbitrary"),
        name="ffn",
    )(x, *norm_args, wg, wu, wd)
    return res if emit_next else res[0]


def _matmul_body(h_ref, w_ref, o_ref):
    o_ref[...] = _dot(h_ref[...], w_ref[...]).astype(o_ref.dtype)


def _matmul(h, w, out_dtype):
    L, K = h.shape
    N = w.shape[1]
    tm = min(1024, L)
    tn = 512
    return pl.pallas_call(
        _matmul_body,
        grid=(L // tm, N // tn),
        in_specs=[pl.BlockSpec((tm, K), lambda i, j: (i, 0)),
                  pl.BlockSpec((K, tn), lambda i, j: (0, j))],
        out_specs=pl.BlockSpec((tm, tn), lambda i, j: (i, j)),
        out_shape=jax.ShapeDtypeStruct((L, N), out_dtype),
        compiler_params=_params("parallel", "arbitrary"),
        name="in_proj",
    )(h, w)


def _win_attn_body(nb, q_ref, kp_ref, kc_ref, kn_ref, vp_ref, vc_ref, vn_ref, bias_ref,
                   sink_ref, o_ref):
    n = pl.program_id(0)
    edge_lo = jnp.where(n == 0, NEG, 0.0)
    edge_hi = jnp.where(n == nb - 1, NEG, 0.0)
    k_all = jnp.concatenate([kp_ref[...], kc_ref[...], kn_ref[...]], axis=0).astype(BF16)
    v_all = jnp.concatenate([vp_ref[...], vc_ref[...], vn_ref[...]], axis=0).astype(BF16)
    low = lax.broadcasted_iota(jnp.int32, (A_BLOCK, LANES), 1) < HEAD_DIM
    tile = lambda i: slice(i * LANES, (i + 1) * LANES)
    outs = []
    for g in range(A_KV_HEADS):
        heads = range(g * GROUP, (g + 1) * GROUP)
        in_kv_half = low if g % 2 == 0 else jnp.logical_not(low)
        qs = []
        for h in heads:
            qh = q_ref[:, tile(h // 2)]
            if h % 2 != g % 2:
                qh = pltpu.roll(qh, HEAD_DIM, axis=1)
            qs.append(jnp.where(in_kv_half, qh, 0.0).astype(BF16))
        s3 = _dot_nt(jnp.concatenate(qs, axis=0), k_all[:, tile(g // 2)])
        ps, dens = [], []
        for j, h in enumerate(heads):
            s = s3[j * A_BLOCK:(j + 1) * A_BLOCK] + bias_ref[h]
            t = [s[:, 0:LANES] + edge_lo, s[:, LANES:2 * LANES], s[:, 2 * LANES:] + edge_hi]
            sink = sink_ref[h]
            m = jnp.maximum(jnp.max(jnp.maximum(jnp.maximum(t[0], t[1]), t[2]), axis=-1,
                                    keepdims=True), sink)
            p = [jnp.exp2(ti - m) for ti in t]
            dens.append(jnp.sum(p[0] + p[1] + p[2], axis=-1, keepdims=True) + jnp.exp2(sink - m))
            ps.append(jnp.concatenate(p, axis=1).astype(BF16))
        o3 = _dot(jnp.concatenate(ps, axis=0), v_all[:, tile(g // 2)])
        for j, h in enumerate(heads):
            o = o3[j * A_BLOCK:(j + 1) * A_BLOCK] / dens[j]
            outs.append(o if h % 2 == g % 2 else pltpu.roll(o, HEAD_DIM, axis=1))
    for pr in range(A_HEADS // 2):
        o_ref[:, tile(pr)] = jnp.where(low, outs[2 * pr], outs[2 * pr + 1]).astype(o_ref.dtype)


def _win_attn(proj, bias, sink):
    L = proj.shape[0]
    nb = L // A_BLOCK
    kcol, vcol = A_Q // A_KV, A_Q // A_KV + 1
    prev = lambda n: jnp.maximum(n - 1, 0)
    nxt = lambda n: jnp.minimum(n + 1, nb - 1)
    kblk = lambda f, c: pl.BlockSpec((A_BLOCK, A_KV), lambda n: (f(n), c))
    same = lambda n: n
    return pl.pallas_call(
        functools.partial(_win_attn_body, nb),
        grid=(nb,),
        in_specs=[pl.BlockSpec((A_BLOCK, A_Q), lambda n: (n, 0)),
                  kblk(prev, kcol), kblk(same, kcol), kblk(nxt, kcol),
                  kblk(prev, vcol), kblk(same, vcol), kblk(nxt, vcol),
                  pl.BlockSpec((A_HEADS, A_BLOCK, 3 * A_BLOCK), lambda n: (0, 0, 0)),
                  pl.BlockSpec((A_HEADS, 1, LANES), lambda n: (0, 0, 0))],
        out_specs=pl.BlockSpec((A_BLOCK, A_Q), lambda n: (n, 0)),
        out_shape=jax.ShapeDtypeStruct((L, A_Q), BF16),
        compiler_params=_params("parallel"),
        name="win_attn",
    )(proj, proj, proj, proj, proj, proj, proj, bias, sink)


def _cumsum_rows(tri, g):
    g1 = g.astype(BF16)
    r1 = g - g1.astype(F32)
    g2 = r1.astype(BF16)
    g3 = (r1 - g2.astype(F32)).astype(BF16)
    return _dot(tri, g1) + _dot(tri, g2) + _dot(tri, g3)


def _hgrn_tables(reverse):
    C = HGRN_CHUNK
    row = lax.broadcasted_iota(jnp.int32, (C, C), 0)
    col = lax.broadcasted_iota(jnp.int32, (C, C), 1)
    tri = ((col >= row) if reverse else (col <= row)).astype(BF16)
    diff = (row ^ col).astype(F32)
    high = (lax.bitcast_convert_type(diff, jnp.int32) >> 23) - 127
    wrong_side = (row < col) if not reverse else (row > col)
    level = jnp.where(row == col, 0, jnp.where(wrong_side, -1, high + 1))
    return tri, level


def _hgrn_chunk(reverse, q, z, v, lb, st_prev, tri, level, later_small):
    C = HGRN_CHUNK
    log_lb = jnp.log(lb)
    log_sig = jnp.minimum(z, 0.0) - jnp.log1p(jnp.exp(-jnp.abs(z)))
    c2 = jnp.log1p(-lb) + log_sig
    g = jnp.maximum(log_lb, c2) + jnp.log1p(jnp.exp(-jnp.abs(log_lb - c2)))
    k = (1.0 - lb) * (1.0 / (1.0 + jnp.exp(z)))

    b = _cumsum_rows(tri, g) * LOG2E
    btot = b[0:1, :] if reverse else b[C - 1:C, :]

    terms = [_dot_nt(q.astype(BF16), k.astype(BF16))]
    seg = b
    for j in range(int(math.log2(C))):
        m = 1 << j
        if m < SUBLANES:
            later = later_small[j] != 0
            up = pltpu.roll(seg, m, axis=0)
            down = pltpu.roll(seg, C - m, axis=0)
            if reverse:
                ref = jnp.where(later, seg, down)
                seg = jnp.where(later, up, seg)
            else:
                ref = jnp.where(later, up, seg)
                seg = jnp.where(later, seg, down)
            x = jnp.exp2(-jnp.abs(b - ref))
        else:
            expo, nxt = [], []
            for p in range(C // (2 * m)):
                lo, hi = slice(2 * p * m, (2 * p + 1) * m), slice((2 * p + 1) * m, (2 * p + 2) * m)
                if reverse:
                    expo += [b[lo] - seg[hi], seg[hi] - b[hi]]
                    nxt += [seg[lo], seg[lo]]
                else:
                    expo += [seg[lo] - b[lo], b[hi] - seg[lo]]
                    nxt += [seg[hi], seg[hi]]
            x = jnp.exp2(jnp.concatenate(expo, axis=0))
            seg = jnp.concatenate(nxt, axis=0)
        terms.append(_dot_nt((q * x).astype(BF16), (k * x).astype(BF16)))
    att = jnp.zeros((C, C), F32)
    for i in reversed(range(len(terms))):
        att = jnp.where(level == i, terms[i], att)
    vb = v.astype(BF16)
    o = _dot(att.astype(BF16), vb)

    o = o + _dot_nt((q * jnp.exp2(b)).astype(BF16), st_prev.astype(BF16))
    kd = (k * jnp.exp2(btot - b)).astype(BF16)
    st_next = jnp.exp2(btot) * st_prev + _dot(v.T.astype(BF16), kd)
    return o, st_next


def _hgrn_body(nch, qf_ref, zf_ref, if_ref, qb_ref, zb_ref, ib_ref, lb_ref, of_ref, ob_ref,
               sf_scr, sb_scr):
    @pl.when(pl.program_id(1) == 0)
    def _():
        sf_scr[...] = jnp.zeros_like(sf_scr)
        sb_scr[...] = jnp.zeros_like(sb_scr)

    C = HGRN_CHUNK
    tables = [_hgrn_tables(False), _hgrn_tables(True)]
    rowv = lax.broadcasted_iota(jnp.int32, (C, LANES), 0)
    later_small = [(rowv >> j) & 1 for j in range(int(math.log2(SUBLANES)))]

    def one(reverse, c, q_ref, z_ref, i_ref, o_ref, s_scr):
        ci = (nch - 1 - c) if reverse else c
        rows = pl.ds(pl.multiple_of(ci * C, C), C)
        qr = q_ref[rows, :]
        tri, level = tables[int(reverse)]
        o, s_next = _hgrn_chunk(reverse, qr * _sigmoid(qr), z_ref[rows, :], i_ref[rows, :],
                                lb_ref[int(reverse), 0], s_scr[...], tri, level, later_small)
        s_scr[...] = s_next
        o_ref[rows, :] = o

    def step(c, carry):
        one(False, c, qf_ref, zf_ref, if_ref, of_ref, sf_scr)
        one(True, c, qb_ref, zb_ref, ib_ref, ob_ref, sb_scr)
        return carry

    lax.fori_loop(0, nch, step, 0, unroll=True)


def _hgrn(proj, lb2):
    proj_b = proj
    L = proj.shape[0]
    tl = min(512, L)
    nt = L // tl
    nch = tl // HGRN_CHUNK
    lb4 = lb2.reshape(2, B_HEADS, 1, B_DK)
    col = lambda part, h: B_COL0 // B_DK + part * B_HEADS + h
    fwd = lambda part: pl.BlockSpec((tl, B_DK), lambda h, t: (t, col(part, h)))
    bwd = lambda part: pl.BlockSpec((tl, B_DK), lambda h, t: (nt - 1 - t, col(part, h)))
    return pl.pallas_call(
        functools.partial(_hgrn_body, nch),
        grid=(B_HEADS, nt),
        in_specs=[fwd(0), fwd(1), fwd(3), bwd(0), bwd(2), bwd(3),
                  pl.BlockSpec((2, 1, 1, B_DK), lambda h, t: (0, h, 0, 0))],
        out_specs=[pl.BlockSpec((tl, B_DV), lambda h, t: (t, h)),
                   pl.BlockSpec((tl, B_DV), lambda h, t: (nt - 1 - t, h))],
        out_shape=[jax.ShapeDtypeStruct((L, B_VW), F32), jax.ShapeDtypeStruct((L, B_VW), F32)],
        scratch_shapes=[pltpu.VMEM((B_DV, B_DK), F32), pltpu.VMEM((B_DV, B_DK), F32)],
        compiler_params=_params("parallel", "arbitrary"),
        name="hgrn",
    )(proj_b, proj_b, proj_b, proj_b, proj_b, proj_b, lb4)


def _rope_prep_body(x_ref, qw_ref, kw_ref, cos_ref, sin_ref, q_ref, kt_ref, v_ref):
    cos = cos_ref[...]
    sin = sin_ref[...]
    lane = lax.broadcasted_iota(jnp.int32, cos.shape, 1)
    even = (lane % 2) == 0
    low = lane < HEAD_DIM
    ones_lane = jnp.where(lane == HEAD_DIM, 1.0, 0.0)
    tile = lambda i: slice(i * LANES, (i + 1) * LANES)
    halves = lambda y: (y, pltpu.roll(y, HEAD_DIM, axis=1))

    def norm_rope(x, w):
        sq = x * x
        ms = jnp.where(low, jnp.sum(jnp.where(low, sq, 0.0), axis=-1, keepdims=True),
                       jnp.sum(jnp.where(low, 0.0, sq), axis=-1, keepdims=True)) * (1.0 / HEAD_DIM)
        y = x * lax.rsqrt(ms + EPS) * w
        partner = jnp.where(even, pltpu.roll(y, LANES - 1, axis=1), pltpu.roll(y, 1, axis=1))
        return y * cos + partner * sin

    for t in range(C_Q // LANES):
        y = norm_rope(x_ref[:, tile(t)], qw_ref[...]) * (LOG2E * HEAD_DIM ** -0.5)
        for half, yh in enumerate(halves(y)):
            q_ref[:, tile(2 * t + half)] = yh.astype(q_ref.dtype)
    for t in range(C_KV // LANES):
        y = norm_rope(x_ref[:, tile(C_Q // LANES + t)], kw_ref[...])
        v = x_ref[:, tile((C_Q + C_KV) // LANES + t)]
        for half, (yh, vh) in enumerate(zip(halves(y), halves(v))):
            kt_ref[tile(2 * t + half), :] = jnp.where(low, yh, 0.0).T.astype(kt_ref.dtype)
            v_ref[:, tile(2 * t + half)] = jnp.where(low, vh, ones_lane).astype(v_ref.dtype)


def _rope_prep(proj, qw, kw, cos_t, sin_t):
    L = proj.shape[0]
    tl = min(256, L)
    width = C_Q + 2 * C_KV
    vec = pl.BlockSpec((1, LANES), lambda i: (0, 0))
    tab = pl.BlockSpec((tl, LANES), lambda i: (i, 0))
    return pl.pallas_call(
        _rope_prep_body,
        grid=(L // tl,),
        in_specs=[pl.BlockSpec((tl, width), lambda i: (i, proj.shape[1] // width - 1)),
                  vec, vec, tab, tab],
        out_specs=[pl.BlockSpec((tl, Q_PAD), lambda i: (i, 0)),
                   pl.BlockSpec((KV_PAD, tl), lambda i: (0, i)),
                   pl.BlockSpec((tl, KV_PAD), lambda i: (i, 0))],
        out_shape=[jax.ShapeDtypeStruct((L, Q_PAD), BF16),
                   jax.ShapeDtypeStruct((KV_PAD, L), BF16),
                   jax.ShapeDtypeStruct((L, KV_PAD), BF16)],
        compiler_params=_params("parallel"),
        name="rope_prep",
    )(proj, qw, kw, cos_t, sin_t)


def _axial_attn_body(tq, tk, nk, q_ref, kt_ref, v_ref, o_ref, m_scr, acc_scr):
    q = jnp.concatenate([q_ref[:, h * LANES:(h + 1) * LANES] for h in range(GROUP)], axis=0)
    m_scr[...] = jnp.full(m_scr.shape, -jnp.inf, F32)
    acc_scr[...] = jnp.zeros(acc_scr.shape, F32)

    def step(t, carry):
        k0 = pl.multiple_of(t * tk, tk)
        s = _dot(q, kt_ref[:, pl.ds(k0, tk)])
        parts = [s[:, i * LANES:(i + 1) * LANES] for i in range(tk // LANES)]
        m = m_scr[...]
        m_new = jnp.maximum(m, jnp.max(functools.reduce(jnp.maximum, parts), axis=-1,
                                       keepdims=True))
        p = jnp.concatenate([jnp.exp2(part - m_new) for part in parts], axis=1)
        acc_scr[...] = jnp.exp2(m - m_new) * acc_scr[...] + _dot(p.astype(BF16),
                                                                  v_ref[pl.ds(k0, tk), :])
        m_scr[...] = m_new
        return carry

    lax.fori_loop(0, nk, step, 0, unroll=AXIAL_UNROLL)
    acc = acc_scr[...]
    lane = lax.broadcasted_iota(jnp.int32, acc.shape, 1)
    o = jnp.where(lane < HEAD_DIM, acc / acc[:, HEAD_DIM:HEAD_DIM + 1], 0.0)
    o0, o1, o2 = o[0:tq], o[tq:2 * tq], o[2 * tq:3 * tq]
    o_ref[:, 0:LANES] = (o0 + pltpu.roll(o1, HEAD_DIM, axis=1)).astype(o_ref.dtype)
    o_ref[:, LANES:2 * LANES] = o2.astype(o_ref.dtype)


def _axial_attn(q, kt, v):
    L = q.shape[0]
    tq = min(256, L)
    tk = min(512, L)
    return pl.pallas_call(
        functools.partial(_axial_attn_body, tq, tk, L // tk),
        grid=(C_KV_HEADS, L // tq),
        in_specs=[pl.BlockSpec((tq, GROUP * LANES), lambda g, i: (i, g)),
                  pl.BlockSpec((LANES, L), lambda g, i: (g, 0)),
                  pl.BlockSpec((L, LANES), lambda g, i: (0, g))],
        out_specs=pl.BlockSpec((tq, 2 * LANES), lambda g, i: (i, g)),
        out_shape=jax.ShapeDtypeStruct((L, C_KV_HEADS * 2 * LANES), BF16),
        scratch_shapes=[pltpu.VMEM((GROUP * tq, LANES), F32), pltpu.VMEM((GROUP * tq, LANES), F32)],
        compiler_params=_params("parallel", "arbitrary"),
        name="axial_attn",
    )(q, kt, v)


def _out_proj_body(x_ref, ya_ref, of_ref, ob_ref, g0_ref, g1_ref, yc_ref, wa_ref, wb_ref, wc_ref,
                   gw_ref, nw_ref, o_ref):
    tot = of_ref[...] + ob_ref[...]
    normed = jnp.concatenate([_rms(tot[:, h * B_DV:(h + 1) * B_DV], gw_ref[...])
                              for h in range(B_HEADS)], axis=1)
    g = jnp.concatenate([g0_ref[...], g1_ref[...]], axis=1)
    yb = (normed * (g * _sigmoid(g))).astype(BF16)
    y = _dot(ya_ref[...], wa_ref[...]) + _dot(yb, wb_ref[...]) + _dot(yc_ref[...], wc_ref[...])
    o_ref[...] = x_ref[...] + _rms(y, nw_ref[...])


def _out_proj(x, ya, o_fwd, o_bwd, proj, yc, wa, wb, wc, gw, nw):
    L = x.shape[0]
    tm = min(512, L)
    row = lambda w: pl.BlockSpec((tm, w), lambda i: (i, 0))
    full = lambda a: pl.BlockSpec(a.shape, lambda i: (0, 0))
    g_col0 = B_COL0 + 3 * B_W + B_VW
    gblk = lambda k: pl.BlockSpec((tm, B_VW // 2), lambda i: (i, g_col0 // (B_VW // 2) + k))
    return pl.pallas_call(
        _out_proj_body,
        grid=(L // tm,),
        in_specs=[row(D_MODEL), row(ya.shape[1]), row(B_VW), row(B_VW), gblk(0), gblk(1),
                  row(yc.shape[1]), full(wa), full(wb), full(wc), full(gw),
                  pl.BlockSpec((1, D_MODEL), lambda i: (0, 0))],
        out_specs=row(D_MODEL),
        out_shape=jax.ShapeDtypeStruct((L, D_MODEL), F32),
        compiler_params=_params("parallel"),
        name="out_proj",
    )(x, ya, o_fwd, o_bwd, proj, proj, yc, wa, wb, wc, gw, nw)


def _split_w_out(w_out):
    wa = w_out[:A_Q]
    wb = w_out[A_Q:A_Q + B_VW]
    wc = w_out[A_Q + B_VW:].reshape(C_KV_HEADS, GROUP * HEAD_DIM, D_MODEL)
    wc = jnp.pad(wc, ((0, 0), (0, 2 * LANES - GROUP * HEAD_DIM), (0, 0)))
    return wa.astype(BF16), wb.astype(BF16), wc.reshape(C_KV_HEADS * 2 * LANES, D_MODEL).astype(BF16)


def _t5_bucket(rel):
    nb = REL_BUCKETS // 2
    max_exact = nb // 2
    n = jnp.abs(rel)
    nf = jnp.maximum(n, 1).astype(F32)
    large = max_exact + (jnp.log(nf / max_exact) / math.log(REL_MAX_DIST / max_exact)
                         * (nb - max_exact)).astype(jnp.int32)
    large = jnp.minimum(large, nb - 1)
    return jnp.where(rel > 0, nb, 0) + jnp.where(n < max_exact, n, large)


def _rope_tables(L):
    half = HEAD_DIM // 2
    inv = 1.0 / (ROPE_THETA ** (jnp.arange(0, half, 2, dtype=F32) / half))
    pos = jnp.arange(L)
    ang = jnp.concatenate([(pos // GRID_W).astype(F32)[:, None] * inv,
                           (pos % GRID_W).astype(F32)[:, None] * inv], axis=-1)
    cos = jnp.repeat(jnp.cos(ang), 2, axis=-1)
    sin = jnp.repeat(jnp.sin(ang), 2, axis=-1) * jnp.tile(jnp.array([-1.0, 1.0], F32), half)
    return jnp.tile(cos, (1, LANES // HEAD_DIM)), jnp.tile(sin, (1, LANES // HEAD_DIM))


def _band_bias(rel_bias):
    period = 4 * A_BLOCK
    rel = jnp.arange(period) - A_BLOCK
    vals = rel_bias.astype(F32)[_t5_bucket(rel)] * LOG2E
    vals = jnp.where((jnp.abs(rel) <= WINDOW)[:, None], vals, NEG).T
    flat = jnp.tile(vals, (1, A_BLOCK))[:, :A_BLOCK * (period - 1)]
    return flat.reshape(A_HEADS, A_BLOCK, period - 1)[:, :, :3 * A_BLOCK]


def kernel(x, w_in, w_out, ffn1_gate, ffn1_up, ffn1_down, ffn2_gate, ffn2_up, ffn2_down,
           norm_w, sink_logits, qk_norm_w, hgrn_lb, hgrn_norm_w, rel_bias):
    B_, L, _ = x.shape
    depth = w_in.shape[0]
    bias = _band_bias(rel_bias)
    cos_t, sin_t = _rope_tables(L)
    lb_c = jnp.cumsum(jax.nn.softmax(hgrn_lb.astype(F32), axis=0), axis=0)
    lbs = lb_c - lb_c[0:1]
    qkw = jnp.tile(qk_norm_w.astype(F32), (1, 1, LANES // HEAD_DIM))
    nw = norm_w.astype(F32).reshape(depth, 6, 1, D_MODEL)
    col_scale = jnp.where(jnp.arange(w_in.shape[-1]) < A_Q, LOG2E * HEAD_DIM ** -0.5, 1.0)
    w_in_b = (w_in * col_scale).astype(BF16)

    outs = []
    for b in range(B_):
        xb = x[b]
        for l in range(depth):
            cast = lambda w: w[l].astype(BF16)
            xb, h = _ffn(xb, nw[l, 0], nw[l, 1], cast(ffn1_gate), cast(ffn1_up),
                         cast(ffn1_down), nextw=nw[l, 2])
            proj = _matmul(h, w_in_b[l], F32)
            ya = _win_attn(proj, bias,
                           jnp.broadcast_to((sink_logits[l].astype(F32) * LOG2E)
                                            .reshape(A_HEADS, 1, 1), (A_HEADS, 1, LANES)))
            o_fwd, o_bwd = _hgrn(proj, lbs[l])
            q, kt, v = _rope_prep(proj, qkw[l, 0:1], qkw[l, 1:2], cos_t, sin_t)
            yc = _axial_attn(q, kt, v)
            oa, ob, oc = _split_w_out(w_out[l])
            xb = _out_proj(xb, ya, o_fwd, o_bwd, proj, yc, oa, ob, oc,
                           hgrn_norm_w[l].astype(F32).reshape(1, B_DV), nw[l, 3])
            xb = _ffn(xb, nw[l, 4], nw[l, 5], cast(ffn2_gate), cast(ffn2_up), cast(ffn2_down))
        outs.append(xb)
    return jnp.stack(outs, axis=0)
```

```python
import functools
import math

import jax
import jax.numpy as jnp
from jax import lax
from jax.experimental import pallas as pl
from jax.experimental.pallas import tpu as pltpu

D_MODEL = 2048
HEAD_DIM = 64
A_HEADS = 12
A_KV_HEADS = 4
WINDOW = 128
A_BLOCK = 128
B_HEADS = 4
B_DK = 128
B_DV = 128
C_HEADS = 12
C_KV_HEADS = 4
ROPE_THETA = 10000.0
GRID_W = 64
REL_BUCKETS = 32
REL_MAX_DIST = 128
D_FF = 5632
EPS = 1e-6
NEG = -1e30

A_Q = A_HEADS * HEAD_DIM
A_KV = A_KV_HEADS * HEAD_DIM
B_W = B_HEADS * B_DK
B_VW = B_HEADS * B_DV
C_Q = C_HEADS * HEAD_DIM
C_KV = C_KV_HEADS * HEAD_DIM
B_COL0 = A_Q + 2 * A_KV

LANES = 128
SUBLANES = 8
Q_PAD = A_HEADS * LANES
KV_PAD = A_KV_HEADS * LANES
GROUP = A_HEADS // A_KV_HEADS
VMEM_LIMIT = 52 * 1024 * 1024
FFN_ROWS = 1024
FFN_COLS = 256
FFN_VMEM_LIMIT = 58 * 1024 * 1024

LOG2E = math.log2(math.e)
AXIAL_UNROLL = 8

HGRN_CHUNK = 128

BF16 = jnp.bfloat16
F32 = jnp.float32


def _params(*sem):
    return pltpu.CompilerParams(dimension_semantics=sem, vmem_limit_bytes=VMEM_LIMIT)


def _dot(a, b):
    return jnp.dot(a, b, preferred_element_type=F32)


def _dot_nt(a, b):
    return lax.dot_general(a, b, (((1,), (1,)), ((), ())), preferred_element_type=F32)


def _rms(x, w):
    ms = jnp.mean(x * x, axis=-1, keepdims=True)
    return x * lax.rsqrt(ms + EPS) * w


def _sigmoid(x):
    return 1.0 / (1.0 + jnp.exp(-x))


def _ffn_body(emit_next, nf, x_ref, prew_ref, postw_ref, *rest):
    if emit_next:
        nextw_ref, wg_ref, wu_ref, wd_ref, o_ref, hn_ref, h_scr = rest
    else:
        wg_ref, wu_ref, wd_ref, o_ref, h_scr = rest
    j = pl.program_id(1)

    @pl.when(j == 0)
    def _():
        h_scr[...] = _rms(x_ref[...], prew_ref[...]).astype(BF16)
        o_ref[...] = jnp.zeros_like(o_ref)

    h = h_scr[...]
    g = _dot(h, wg_ref[...].astype(BF16))
    u = _dot(h, wu_ref[...].astype(BF16))
    a = (g * _sigmoid(g)) * u
    o_ref[...] += _dot(a.astype(BF16), wd_ref[...].astype(BF16))

    @pl.when(j == nf - 1)
    def _():
        xn = x_ref[...] + 0.5 * _rms(o_ref[...], postw_ref[...])
        o_ref[...] = xn
        if emit_next:
            hn_ref[...] = _rms(xn, nextw_ref[...]).astype(BF16)


def _ffn(x, prew, postw, wg, wu, wd, layer, nextw=None):
    emit_next = nextw is not None
    L = x.shape[0]
    tm = min(FFN_ROWS, L)
    tf = FFN_COLS
    nf = D_FF // tf
    row = lambda i, j: (i, 0)
    vec = pl.BlockSpec((1, D_MODEL), lambda i, j: (0, 0))
    out_shape = [jax.ShapeDtypeStruct((L, D_MODEL), F32)]
    out_specs = [pl.BlockSpec((tm, D_MODEL), row)]
    x_spec = pl.BlockSpec((tm, D_MODEL), row)
    if emit_next:
        out_shape.append(jax.ShapeDtypeStruct((L, D_MODEL), BF16))
        out_specs.append(pl.BlockSpec((tm, D_MODEL), row))
        x_spec = pl.BlockSpec((tm, D_MODEL), row, pipeline_mode=pl.Buffered(1))
    norm_args = [prew, postw] + ([nextw] if emit_next else [])
    res = pl.pallas_call(
        functools.partial(_ffn_body, emit_next, nf),
        grid=(L // tm, nf),
        in_specs=[x_spec] + [vec] * len(norm_args) + [
            pl.BlockSpec((None, D_MODEL, tf), lambda i, j: (layer, 0, j)),
            pl.BlockSpec((None, D_MODEL, tf), lambda i, j: (layer, 0, j)),
            pl.BlockSpec((None, tf, D_MODEL), lambda i, j: (layer, j, 0))],
        out_specs=out_specs,
        out_shape=out_shape,
        scratch_shapes=[pltpu.VMEM((tm, D_MODEL), BF16)],
        compiler_params=pltpu.CompilerParams(dimension_semantics=("parallel", "arbitrary"),
                                             vmem_limit_bytes=FFN_VMEM_LIMIT),
        name="ffn",
    )(x, *norm_args, wg, wu, wd)
    return res if emit_next else res[0]


def _matmul_body(h_ref, w_ref, o_ref):
    o_ref[...] = _dot(h_ref[...], w_ref[...].astype(BF16)).astype(o_ref.dtype)


def _matmul(h, w, layer, out_dtype):
    L, K = h.shape
    N = w.shape[2]
    tm = min(1024, L)
    tn = 512
    return pl.pallas_call(
        _matmul_body,
        grid=(L // tm, N // tn),
        in_specs=[pl.BlockSpec((tm, K), lambda i, j: (i, 0)),
                  pl.BlockSpec((None, K, tn), lambda i, j: (layer, 0, j))],
        out_specs=pl.BlockSpec((tm, tn), lambda i, j: (i, j)),
        out_shape=jax.ShapeDtypeStruct((L, N), out_dtype),
        compiler_params=_params("parallel", "arbitrary"),
        name="in_proj",
    )(h, w)


def _win_attn_body(nb, q_ref, kp_ref, kc_ref, kn_ref, vp_ref, vc_ref, vn_ref, bias_ref,
                   sink_ref, o_ref):
    n = pl.program_id(0)
    edge_lo = jnp.where(n == 0, NEG, 0.0)
    edge_hi = jnp.where(n == nb - 1, NEG, 0.0)
    k_all = jnp.concatenate([kp_ref[...], kc_ref[...], kn_ref[...]], axis=0).astype(BF16)
    v_all = jnp.concatenate([vp_ref[...], vc_ref[...], vn_ref[...]], axis=0).astype(BF16)
    low = lax.broadcasted_iota(jnp.int32, (A_BLOCK, LANES), 1) < HEAD_DIM
    tile = lambda i: slice(i * LANES, (i + 1) * LANES)
    outs = []
    for g in range(A_KV_HEADS):
        heads = range(g * GROUP, (g + 1) * GROUP)
        in_kv_half = low if g % 2 == 0 else jnp.logical_not(low)
        qs = []
        for h in heads:
            qh = q_ref[:, tile(h // 2)] * (LOG2E * HEAD_DIM ** -0.5)
            if h % 2 != g % 2:
                qh = pltpu.roll(qh, HEAD_DIM, axis=1)
            qs.append(jnp.where(in_kv_half, qh, 0.0).astype(BF16))
        s3 = _dot_nt(jnp.concatenate(qs, axis=0), k_all[:, tile(g // 2)])
        ps, dens = [], []
        for j, h in enumerate(heads):
            s = s3[j * A_BLOCK:(j + 1) * A_BLOCK] + bias_ref[h]
            t = [s[:, 0:LANES] + edge_lo, s[:, LANES:2 * LANES], s[:, 2 * LANES:] + edge_hi]
            sink = sink_ref[h]
            m = jnp.maximum(jnp.max(jnp.maximum(jnp.maximum(t[0], t[1]), t[2]), axis=-1,
                                    keepdims=True), sink)
            p = [jnp.exp2(ti - m) for ti in t]
            dens.append(jnp.sum(p[0] + p[1] + p[2], axis=-1, keepdims=True) + jnp.exp2(sink - m))
            ps.append(jnp.concatenate(p, axis=1).astype(BF16))
        o3 = _dot(jnp.concatenate(ps, axis=0), v_all[:, tile(g // 2)])
        for j, h in enumerate(heads):
            o = o3[j * A_BLOCK:(j + 1) * A_BLOCK] / dens[j]
            outs.append(o if h % 2 == g % 2 else pltpu.roll(o, HEAD_DIM, axis=1))
    for pr in range(A_HEADS // 2):
        o_ref[:, tile(pr)] = jnp.where(low, outs[2 * pr], outs[2 * pr + 1]).astype(o_ref.dtype)


def _win_attn(proj, bias, sink):
    L = proj.shape[0]
    nb = L // A_BLOCK
    kcol, vcol = A_Q // A_KV, A_Q // A_KV + 1
    prev = lambda n: jnp.maximum(n - 1, 0)
    nxt = lambda n: jnp.minimum(n + 1, nb - 1)
    kblk = lambda f, c: pl.BlockSpec((A_BLOCK, A_KV), lambda n: (f(n), c))
    same = lambda n: n
    return pl.pallas_call(
        functools.partial(_win_attn_body, nb),
        grid=(nb,),
        in_specs=[pl.BlockSpec((A_BLOCK, A_Q), lambda n: (n, 0)),
                  kblk(prev, kcol), kblk(same, kcol), kblk(nxt, kcol),
                  kblk(prev, vcol), kblk(same, vcol), kblk(nxt, vcol),
                  pl.BlockSpec((A_HEADS, A_BLOCK, 3 * A_BLOCK), lambda n: (0, 0, 0)),
                  pl.BlockSpec((A_HEADS, 1, LANES), lambda n: (0, 0, 0))],
        out_specs=pl.BlockSpec((A_BLOCK, A_Q), lambda n: (n, 0)),
        out_shape=jax.ShapeDtypeStruct((L, A_Q), BF16),
        compiler_params=_params("parallel"),
        name="win_attn",
    )(proj, proj, proj, proj, proj, proj, proj, bias, sink)


def _cumsum_rows(tri, g):
    g1 = g.astype(BF16)
    r1 = g - g1.astype(F32)
    g2 = r1.astype(BF16)
    g3 = (r1 - g2.astype(F32)).astype(BF16)
    return _dot(tri, g1) + _dot(tri, g2) + _dot(tri, g3)


def _hgrn_tables(reverse):
    C = HGRN_CHUNK
    row = lax.broadcasted_iota(jnp.int32, (C, C), 0)
    col = lax.broadcasted_iota(jnp.int32, (C, C), 1)
    tri = ((col >= row) if reverse else (col <= row)).astype(BF16)
    diff = (row ^ col).astype(F32)
    high = (lax.bitcast_convert_type(diff, jnp.int32) >> 23) - 127
    wrong_side = (row < col) if not reverse else (row > col)
    level = jnp.where(row == col, 0, jnp.where(wrong_side, -1, high + 1))
    return tri, level


def _hgrn_chunk(reverse, q, z, v, lb, st_prev, tri, level, later_small):
    C = HGRN_CHUNK
    log_lb = jnp.log(lb)
    log_sig = jnp.minimum(z, 0.0) - jnp.log(1.0 + jnp.exp(-jnp.abs(z)))
    c2 = jnp.log1p(-lb) + log_sig
    g = jnp.maximum(log_lb, c2) + jnp.log(1.0 + jnp.exp(-jnp.abs(log_lb - c2)))
    k = (1.0 - lb) * (1.0 / (1.0 + jnp.exp(z)))

    b = _cumsum_rows(tri, g) * LOG2E
    btot = b[0:1, :] if reverse else b[C - 1:C, :]

    terms = [_dot_nt(q.astype(BF16), k.astype(BF16))]
    seg = b
    for j in range(int(math.log2(C))):
        m = 1 << j
        if m < SUBLANES:
            later = later_small[j] != 0
            up = pltpu.roll(seg, m, axis=0)
            down = pltpu.roll(seg, C - m, axis=0)
            if reverse:
                ref = jnp.where(later, seg, down)
                seg = jnp.where(later, up, seg)
            else:
                ref = jnp.where(later, up, seg)
                seg = jnp.where(later, seg, down)
            x = jnp.exp2(-jnp.abs(b - ref))
        else:
            expo, nxt = [], []
            for p in range(C // (2 * m)):
                lo, hi = slice(2 * p * m, (2 * p + 1) * m), slice((2 * p + 1) * m, (2 * p + 2) * m)
                if reverse:
                    expo += [b[lo] - seg[hi], seg[hi] - b[hi]]
                    nxt += [seg[lo], seg[lo]]
                else:
                    expo += [seg[lo] - b[lo], b[hi] - seg[lo]]
                    nxt += [seg[hi], seg[hi]]
            x = jnp.exp2(jnp.concatenate(expo, axis=0))
            seg = jnp.concatenate(nxt, axis=0)
        terms.append(_dot_nt((q * x).astype(BF16), (k * x).astype(BF16)))
    att = jnp.zeros((C, C), F32)
    for i in reversed(range(len(terms))):
        att = jnp.where(level == i, terms[i], att)
    vb = v.astype(BF16)
    o = _dot(att.astype(BF16), vb)

    o = o + _dot_nt((q * jnp.exp2(b)).astype(BF16), st_prev.astype(BF16))
    kd = (k * jnp.exp2(btot - b)).astype(BF16)
    st_next = jnp.exp2(btot) * st_prev + _dot(v.T.astype(BF16), kd)
    return o, st_next


def _hgrn_body(nch, qf_ref, zf_ref, if_ref, qb_ref, zb_ref, ib_ref, lb_ref, of_ref, ob_ref,
               sf_scr, sb_scr):
    @pl.when(pl.program_id(1) == 0)
    def _():
        sf_scr[...] = jnp.zeros_like(sf_scr)
        sb_scr[...] = jnp.zeros_like(sb_scr)

    C = HGRN_CHUNK
    tables = [_hgrn_tables(False), _hgrn_tables(True)]
    rowv = lax.broadcasted_iota(jnp.int32, (C, LANES), 0)
    later_small = [(rowv >> j) & 1 for j in range(int(math.log2(SUBLANES)))]

    def one(reverse, c, q_ref, z_ref, i_ref, o_ref, s_scr):
        ci = (nch - 1 - c) if reverse else c
        rows = pl.ds(pl.multiple_of(ci * C, C), C)
        qr = q_ref[rows, :]
        tri, level = tables[int(reverse)]
        o, s_next = _hgrn_chunk(reverse, qr * _sigmoid(qr), z_ref[rows, :], i_ref[rows, :],
                                lb_ref[int(reverse), 0], s_scr[...], tri, level, later_small)
        s_scr[...] = s_next
        o_ref[rows, :] = o

    def step(c, carry):
        one(False, c, qf_ref, zf_ref, if_ref, of_ref, sf_scr)
        one(True, c, qb_ref, zb_ref, ib_ref, ob_ref, sb_scr)
        return carry

    lax.fori_loop(0, nch, step, 0, unroll=True)


def _hgrn(proj, lb2):
    proj_b = proj
    L = proj.shape[0]
    tl = min(512, L)
    nt = L // tl
    nch = tl // HGRN_CHUNK
    lb4 = lb2.reshape(2, B_HEADS, 1, B_DK)
    col = lambda part, h: B_COL0 // B_DK + part * B_HEADS + h
    fwd = lambda part: pl.BlockSpec((tl, B_DK), lambda h, t: (t, col(part, h)))
    bwd = lambda part: pl.BlockSpec((tl, B_DK), lambda h, t: (nt - 1 - t, col(part, h)))
    return pl.pallas_call(
        functools.partial(_hgrn_body, nch),
        grid=(B_HEADS, nt),
        in_specs=[fwd(0), fwd(1), fwd(3), bwd(0), bwd(2), bwd(3),
                  pl.BlockSpec((2, 1, 1, B_DK), lambda h, t: (0, h, 0, 0))],
        out_specs=[pl.BlockSpec((tl, B_DV), lambda h, t: (t, h)),
                   pl.BlockSpec((tl, B_DV), lambda h, t: (nt - 1 - t, h))],
        out_shape=[jax.ShapeDtypeStruct((L, B_VW), F32), jax.ShapeDtypeStruct((L, B_VW), F32)],
        scratch_shapes=[pltpu.VMEM((B_DV, B_DK), F32), pltpu.VMEM((B_DV, B_DK), F32)],
        compiler_params=_params("parallel", "arbitrary"),
        name="hgrn",
    )(proj_b, proj_b, proj_b, proj_b, proj_b, proj_b, lb4)


def _rope_prep_body(x_ref, qw_ref, kw_ref, cos_ref, sin_ref, q_ref, kt_ref, v_ref):
    cos = cos_ref[...]
    sin = sin_ref[...]
    lane = lax.broadcasted_iota(jnp.int32, cos.shape, 1)
    even = (lane % 2) == 0
    low = lane < HEAD_DIM
    ones_lane = jnp.where(lane == HEAD_DIM, 1.0, 0.0)
    tile = lambda i: slice(i * LANES, (i + 1) * LANES)
    halves = lambda y: (y, pltpu.roll(y, HEAD_DIM, axis=1))

    def norm_rope(x, w):
        sq = x * x
        ms = jnp.where(low, jnp.sum(jnp.where(low, sq, 0.0), axis=-1, keepdims=True),
                       jnp.sum(jnp.where(low, 0.0, sq), axis=-1, keepdims=True)) * (1.0 / HEAD_DIM)
        y = x * lax.rsqrt(ms + EPS) * w
        partner = jnp.where(even, pltpu.roll(y, LANES - 1, axis=1), pltpu.roll(y, 1, axis=1))
        return y * cos + partner * sin

    for t in range(C_Q // LANES):
        y = norm_rope(x_ref[:, tile(t)], qw_ref[...]) * (LOG2E * HEAD_DIM ** -0.5)
        for half, yh in enumerate(halves(y)):
            q_ref[:, tile(2 * t + half)] = yh.astype(q_ref.dtype)
    for t in range(C_KV // LANES):
        y = norm_rope(x_ref[:, tile(C_Q // LANES + t)], kw_ref[...])
        v = x_ref[:, tile((C_Q + C_KV) // LANES + t)]
        for half, (yh, vh) in enumerate(zip(halves(y), halves(v))):
            kt_ref[tile(2 * t + half), :] = jnp.where(low, yh, 0.0).T.astype(kt_ref.dtype)
            v_ref[:, tile(2 * t + half)] = jnp.where(low, vh, ones_lane).astype(v_ref.dtype)


def _rope_prep(proj, qw, kw, cos_t, sin_t):
    L = proj.shape[0]
    tl = min(256, L)
    width = C_Q + 2 * C_KV
    vec = pl.BlockSpec((1, LANES), lambda i: (0, 0))
    tab = pl.BlockSpec((tl, LANES), lambda i: (i, 0))
    return pl.pallas_call(
        _rope_prep_body,
        grid=(L // tl,),
        in_specs=[pl.BlockSpec((tl, width), lambda i: (i, proj.shape[1] // width - 1)),
                  vec, vec, tab, tab],
        out_specs=[pl.BlockSpec((tl, Q_PAD), lambda i: (i, 0)),
                   pl.BlockSpec((KV_PAD, tl), lambda i: (0, i)),
                   pl.BlockSpec((tl, KV_PAD), lambda i: (i, 0))],
        out_shape=[jax.ShapeDtypeStruct((L, Q_PAD), BF16),
                   jax.ShapeDtypeStruct((KV_PAD, L), BF16),
                   jax.ShapeDtypeStruct((L, KV_PAD), BF16)],
        compiler_params=_params("parallel"),
        name="rope_prep",
    )(proj, qw, kw, cos_t, sin_t)


def _axial_attn_body(tq, tk, nk, q_ref, kt_ref, v_ref, o_ref, m_scr, acc_scr):
    q = jnp.concatenate([q_ref[:, h * LANES:(h + 1) * LANES] for h in range(GROUP)], axis=0)
    m_scr[...] = jnp.full(m_scr.shape, -jnp.inf, F32)
    acc_scr[...] = jnp.zeros(acc_scr.shape, F32)

    def step(t, carry):
        k0 = pl.multiple_of(t * tk, tk)
        s = _dot(q, kt_ref[:, pl.ds(k0, tk)])
        parts = [s[:, i * LANES:(i + 1) * LANES] for i in range(tk // LANES)]
        m = m_scr[...]
        m_new = jnp.maximum(m, jnp.max(functools.reduce(jnp.maximum, parts), axis=-1,
                                       keepdims=True))
        p = jnp.concatenate([jnp.exp2(part - m_new) for part in parts], axis=1)
        acc_scr[...] = jnp.exp2(m - m_new) * acc_scr[...] + _dot(p.astype(BF16),
                                                                  v_ref[pl.ds(k0, tk), :])
        m_scr[...] = m_new
        return carry

    lax.fori_loop(0, nk, step, 0, unroll=AXIAL_UNROLL)
    acc = acc_scr[...]
    lane = lax.broadcasted_iota(jnp.int32, acc.shape, 1)
    o = jnp.where(lane < HEAD_DIM, acc / acc[:, HEAD_DIM:HEAD_DIM + 1], 0.0)
    o0, o1, o2 = o[0:tq], o[tq:2 * tq], o[2 * tq:3 * tq]
    o_ref[:, 0:LANES] = (o0 + pltpu.roll(o1, HEAD_DIM, axis=1)).astype(o_ref.dtype)
    o_ref[:, LANES:2 * LANES] = o2.astype(o_ref.dtype)


def _axial_attn(q, kt, v):
    L = q.shape[0]
    tq = min(256, L)
    tk = min(512, L)
    return pl.pallas_call(
        functools.partial(_axial_attn_body, tq, tk, L // tk),
        grid=(C_KV_HEADS, L // tq),
        in_specs=[pl.BlockSpec((tq, GROUP * LANES), lambda g, i: (i, g)),
                  pl.BlockSpec((LANES, L), lambda g, i: (g, 0)),
                  pl.BlockSpec((L, LANES), lambda g, i: (0, g))],
        out_specs=pl.BlockSpec((tq, 2 * LANES), lambda g, i: (i, g)),
        out_shape=jax.ShapeDtypeStruct((L, C_KV_HEADS * 2 * LANES), BF16),
        scratch_shapes=[pltpu.VMEM((GROUP * tq, LANES), F32), pltpu.VMEM((GROUP * tq, LANES), F32)],
        compiler_params=_params("parallel", "arbitrary"),
        name="axial_attn",
    )(q, kt, v)


def _out_proj_body(x_ref, ya_ref, of_ref, ob_ref, g0_ref, g1_ref, yc_ref, wa_ref, wb_ref, wc_ref,
                   gw_ref, nw_ref, o_ref):
    tot = of_ref[...] + ob_ref[...]
    normed = jnp.concatenate([_rms(tot[:, h * B_DV:(h + 1) * B_DV], gw_ref[...])
                              for h in range(B_HEADS)], axis=1)
    g = jnp.concatenate([g0_ref[...], g1_ref[...]], axis=1)
    yb = (normed * (g * _sigmoid(g))).astype(BF16)
    y = _dot(ya_ref[...], wa_ref[...]) + _dot(yb, wb_ref[...]) + _dot(yc_ref[...], wc_ref[...])
    o_ref[...] = x_ref[...] + _rms(y, nw_ref[...])


def _out_proj(x, ya, o_fwd, o_bwd, proj, yc, wa, wb, wc, gw, nw):
    L = x.shape[0]
    tm = min(512, L)
    row = lambda w: pl.BlockSpec((tm, w), lambda i: (i, 0))
    full = lambda a: pl.BlockSpec(a.shape, lambda i: (0, 0))
    g_col0 = B_COL0 + 3 * B_W + B_VW
    gblk = lambda k: pl.BlockSpec((tm, B_VW // 2), lambda i: (i, g_col0 // (B_VW // 2) + k))
    return pl.pallas_call(
        _out_proj_body,
        grid=(L // tm,),
        in_specs=[row(D_MODEL), row(ya.shape[1]), row(B_VW), row(B_VW), gblk(0), gblk(1),
                  row(yc.shape[1]), full(wa), full(wb), full(wc), full(gw),
                  pl.BlockSpec((1, D_MODEL), lambda i: (0, 0))],
        out_specs=row(D_MODEL),
        out_shape=jax.ShapeDtypeStruct((L, D_MODEL), F32),
        compiler_params=_params("parallel"),
        name="out_proj",
    )(x, ya, o_fwd, o_bwd, proj, proj, yc, wa, wb, wc, gw, nw)


def _split_w_out(w_out):
    wa = w_out[:A_Q]
    wb = w_out[A_Q:A_Q + B_VW]
    wc = w_out[A_Q + B_VW:].reshape(C_KV_HEADS, GROUP * HEAD_DIM, D_MODEL)
    wc = jnp.pad(wc, ((0, 0), (0, 2 * LANES - GROUP * HEAD_DIM), (0, 0)))
    return wa.astype(BF16), wb.astype(BF16), wc.reshape(C_KV_HEADS * 2 * LANES, D_MODEL).astype(BF16)


def _t5_bucket(rel):
    nb = REL_BUCKETS // 2
    max_exact = nb // 2
    n = jnp.abs(rel)
    nf = jnp.maximum(n, 1).astype(F32)
    large = max_exact + (jnp.log(nf / max_exact) / math.log(REL_MAX_DIST / max_exact)
                         * (nb - max_exact)).astype(jnp.int32)
    large = jnp.minimum(large, nb - 1)
    return jnp.where(rel > 0, nb, 0) + jnp.where(n < max_exact, n, large)


def _rope_tables(L):
    half = HEAD_DIM // 2
    inv = 1.0 / (ROPE_THETA ** (jnp.arange(0, half, 2, dtype=F32) / half))
    pos = jnp.arange(L)
    ang = jnp.concatenate([(pos // GRID_W).astype(F32)[:, None] * inv,
                           (pos % GRID_W).astype(F32)[:, None] * inv], axis=-1)
    cos = jnp.repeat(jnp.cos(ang), 2, axis=-1)
    sin = jnp.repeat(jnp.sin(ang), 2, axis=-1) * jnp.tile(jnp.array([-1.0, 1.0], F32), half)
    return jnp.tile(cos, (1, LANES // HEAD_DIM)), jnp.tile(sin, (1, LANES // HEAD_DIM))


def _band_bias(rel_bias):
    period = 4 * A_BLOCK
    rel = jnp.arange(period) - A_BLOCK
    vals = rel_bias.astype(F32)[_t5_bucket(rel)] * LOG2E
    vals = jnp.where((jnp.abs(rel) <= WINDOW)[:, None], vals, NEG).T
    flat = jnp.tile(vals, (1, A_BLOCK))[:, :A_BLOCK * (period - 1)]
    return flat.reshape(A_HEADS, A_BLOCK, period - 1)[:, :, :3 * A_BLOCK]


def kernel(x, w_in, w_out, ffn1_gate, ffn1_up, ffn1_down, ffn2_gate, ffn2_up, ffn2_down,
           norm_w, sink_logits, qk_norm_w, hgrn_lb, hgrn_norm_w, rel_bias):
    B_, L, _ = x.shape
    depth = w_in.shape[0]
    bias = _band_bias(rel_bias)
    cos_t, sin_t = _rope_tables(L)
    lb_c = jnp.cumsum(jax.nn.softmax(hgrn_lb.astype(F32), axis=0), axis=0)
    lbs = lb_c - lb_c[0:1]
    qkw = jnp.tile(qk_norm_w.astype(F32), (1, 1, LANES // HEAD_DIM))
    nw = norm_w.astype(F32).reshape(depth, 6, 1, D_MODEL)
    ffn1_b = (ffn1_gate, ffn1_up, ffn1_down)
    ffn2_b = (ffn2_gate, ffn2_up, ffn2_down)

    outs = []
    for b in range(B_):
        xb = x[b]
        for l in range(depth):
            xb, h = _ffn(xb, nw[l, 0], nw[l, 1], *ffn1_b, l, nextw=nw[l, 2])
            proj = _matmul(h, w_in, l, F32)
            ya = _win_attn(proj, bias,
                           jnp.broadcast_to((sink_logits[l].astype(F32) * LOG2E)
                                            .reshape(A_HEADS, 1, 1), (A_HEADS, 1, LANES)))
            o_fwd, o_bwd = _hgrn(proj, lbs[l])
            q, kt, v = _rope_prep(proj, qkw[l, 0:1], qkw[l, 1:2], cos_t, sin_t)
            yc = _axial_attn(q, kt, v)
            oa, ob, oc = _split_w_out(w_out[l])
            xb = _out_proj(xb, ya, o_fwd, o_bwd, proj, yc, oa, ob, oc,
                           hgrn_norm_w[l].astype(F32).reshape(1, B_DV), nw[l, 3])
            xb = _ffn(xb, nw[l, 4], nw[l, 5], *ffn2_b, l)
        outs.append(xb)
    return jnp.stack(outs, axis=0)
```

```python
import functools
import math

import jax
import jax.numpy as jnp
from jax import lax
from jax.experimental import pallas as pl
from jax.experimental.pallas import tpu as pltpu

D_MODEL = 2048
HEAD_DIM = 64
A_HEADS = 12
A_KV_HEADS = 4
WINDOW = 128
A_BLOCK = 128
B_HEADS = 4
B_DK = 128
B_DV = 128
C_HEADS = 12
C_KV_HEADS = 4
ROPE_THETA = 10000.0
GRID_W = 64
REL_BUCKETS = 32
REL_MAX_DIST = 128
D_FF = 5632
EPS = 1e-6
NEG = -1e30

A_Q = A_HEADS * HEAD_DIM
A_KV = A_KV_HEADS * HEAD_DIM
B_W = B_HEADS * B_DK
B_VW = B_HEADS * B_DV
C_Q = C_HEADS * HEAD_DIM
C_KV = C_KV_HEADS * HEAD_DIM
B_COL0 = A_Q + 2 * A_KV

LANES = 128
SUBLANES = 8
Q_PAD = A_HEADS * LANES
KV_PAD = A_KV_HEADS * LANES
GROUP = A_HEADS // A_KV_HEADS
VMEM_LIMIT = 52 * 1024 * 1024
FFN_ROWS = 1024
FFN_COLS = 256
FFN_VMEM_LIMIT = 58 * 1024 * 1024

LOG2E = math.log2(math.e)
AXIAL_UNROLL = 8

HGRN_CHUNK = 128
HGRN_HEADS_PER_STEP = 2

BF16 = jnp.bfloat16
F32 = jnp.float32


def _params(*sem):
    return pltpu.CompilerParams(dimension_semantics=sem, vmem_limit_bytes=VMEM_LIMIT)


def _dot(a, b):
    return jnp.dot(a, b, preferred_element_type=F32)


def _dot_nt(a, b):
    return lax.dot_general(a, b, (((1,), (1,)), ((), ())), preferred_element_type=F32)


def _rms(x, w):
    ms = jnp.mean(x * x, axis=-1, keepdims=True)
    return x * lax.rsqrt(ms + EPS) * w


def _sigmoid(x):
    return 1.0 / (1.0 + jnp.exp(-x))


def _ffn_body(nf, x_ref, prew_ref, postw_ref, wg_ref, wu_ref, wd_ref, o_ref, h_scr):
    j = pl.program_id(1)

    @pl.when(j == 0)
    def _():
        h_scr[...] = _rms(x_ref[...], prew_ref[...]).astype(BF16)
        o_ref[...] = jnp.zeros_like(o_ref)

    h = h_scr[...]
    g = _dot(h, wg_ref[...].astype(BF16))
    u = _dot(h, wu_ref[...].astype(BF16))
    a = (g * _sigmoid(g)) * u
    o_ref[...] += _dot(a.astype(BF16), wd_ref[...].astype(BF16))

    @pl.when(j == nf - 1)
    def _():
        o_ref[...] = x_ref[...] + 0.5 * _rms(o_ref[...], postw_ref[...])


def _ffn(x, prew, postw, wg, wu, wd, layer):
    L = x.shape[0]
    tm = min(FFN_ROWS, L)
    tf = FFN_COLS
    nf = D_FF // tf
    row = pl.BlockSpec((tm, D_MODEL), lambda i, j: (i, 0))
    vec = pl.BlockSpec((1, D_MODEL), lambda i, j: (0, 0))
    return pl.pallas_call(
        functools.partial(_ffn_body, nf),
        grid=(L // tm, nf),
        in_specs=[row, vec, vec,
                  pl.BlockSpec((None, D_MODEL, tf), lambda i, j: (layer, 0, j)),
                  pl.BlockSpec((None, D_MODEL, tf), lambda i, j: (layer, 0, j)),
                  pl.BlockSpec((None, tf, D_MODEL), lambda i, j: (layer, j, 0))],
        out_specs=row,
        out_shape=jax.ShapeDtypeStruct((L, D_MODEL), F32),
        scratch_shapes=[pltpu.VMEM((tm, D_MODEL), BF16)],
        compiler_params=pltpu.CompilerParams(dimension_semantics=("parallel", "arbitrary"),
                                             vmem_limit_bytes=FFN_VMEM_LIMIT),
        name="ffn",
    )(x, prew, postw, wg, wu, wd)


def _in_proj_body(x_ref, nw_ref, w_ref, o_ref, h_scr):
    @pl.when(pl.program_id(1) == 0)
    def _():
        h_scr[...] = _rms(x_ref[...], nw_ref[...]).astype(BF16)

    o_ref[...] = _dot(h_scr[...], w_ref[...].astype(BF16))


def _in_proj(x, nw, w, layer):
    L = x.shape[0]
    N = w.shape[2]
    tm = min(1024, L)
    tn = 512
    return pl.pallas_call(
        _in_proj_body,
        grid=(L // tm, N // tn),
        in_specs=[pl.BlockSpec((tm, D_MODEL), lambda i, j: (i, 0)),
                  pl.BlockSpec((1, D_MODEL), lambda i, j: (0, 0)),
                  pl.BlockSpec((None, D_MODEL, tn), lambda i, j: (layer, 0, j))],
        out_specs=pl.BlockSpec((tm, tn), lambda i, j: (i, j)),
        out_shape=jax.ShapeDtypeStruct((L, N), F32),
        scratch_shapes=[pltpu.VMEM((tm, D_MODEL), BF16)],
        compiler_params=_params("parallel", "arbitrary"),
        name="in_proj",
    )(x, nw, w)


def _win_attn_body(nb, q_ref, kp_ref, kc_ref, kn_ref, vp_ref, vc_ref, vn_ref, bias_ref,
                   sink_ref, o_ref):
    n = pl.program_id(0)
    edge_lo = jnp.where(n == 0, NEG, 0.0)
    edge_hi = jnp.where(n == nb - 1, NEG, 0.0)
    k_all = jnp.concatenate([kp_ref[...], kc_ref[...], kn_ref[...]], axis=0).astype(BF16)
    v_all = jnp.concatenate([vp_ref[...], vc_ref[...], vn_ref[...]], axis=0).astype(BF16)
    low = lax.broadcasted_iota(jnp.int32, (A_BLOCK, LANES), 1) < HEAD_DIM
    tile = lambda i: slice(i * LANES, (i + 1) * LANES)
    outs = []
    for g in range(A_KV_HEADS):
        heads = range(g * GROUP, (g + 1) * GROUP)
        in_kv_half = low if g % 2 == 0 else jnp.logical_not(low)
        qs = []
        for h in heads:
            qh = q_ref[:, tile(h // 2)] * (LOG2E * HEAD_DIM ** -0.5)
            if h % 2 != g % 2:
                qh = pltpu.roll(qh, HEAD_DIM, axis=1)
            qs.append(jnp.where(in_kv_half, qh, 0.0).astype(BF16))
        s3 = _dot_nt(jnp.concatenate(qs, axis=0), k_all[:, tile(g // 2)])
        ps, dens = [], []
        for j, h in enumerate(heads):
            s = s3[j * A_BLOCK:(j + 1) * A_BLOCK] + bias_ref[h]
            t = [s[:, 0:LANES] + edge_lo, s[:, LANES:2 * LANES], s[:, 2 * LANES:] + edge_hi]
            sink = sink_ref[h]
            m = jnp.maximum(jnp.max(jnp.maximum(jnp.maximum(t[0], t[1]), t[2]), axis=-1,
                                    keepdims=True), sink)
            p = [jnp.exp2(ti - m) for ti in t]
            dens.append(jnp.sum(p[0] + p[1] + p[2], axis=-1, keepdims=True) + jnp.exp2(sink - m))
            ps.append(jnp.concatenate(p, axis=1).astype(BF16))
        o3 = _dot(jnp.concatenate(ps, axis=0), v_all[:, tile(g // 2)])
        for j, h in enumerate(heads):
            o = o3[j * A_BLOCK:(j + 1) * A_BLOCK] / dens[j]
            outs.append(o if h % 2 == g % 2 else pltpu.roll(o, HEAD_DIM, axis=1))
    for pr in range(A_HEADS // 2):
        o_ref[:, tile(pr)] = jnp.where(low, outs[2 * pr], outs[2 * pr + 1]).astype(o_ref.dtype)


def _win_attn(proj, bias, sink):
    L = proj.shape[0]
    nb = L // A_BLOCK
    kcol, vcol = A_Q // A_KV, A_Q // A_KV + 1
    prev = lambda n: jnp.maximum(n - 1, 0)
    nxt = lambda n: jnp.minimum(n + 1, nb - 1)
    kblk = lambda f, c: pl.BlockSpec((A_BLOCK, A_KV), lambda n: (f(n), c))
    same = lambda n: n
    return pl.pallas_call(
        functools.partial(_win_attn_body, nb),
        grid=(nb,),
        in_specs=[pl.BlockSpec((A_BLOCK, A_Q), lambda n: (n, 0)),
                  kblk(prev, kcol), kblk(same, kcol), kblk(nxt, kcol),
                  kblk(prev, vcol), kblk(same, vcol), kblk(nxt, vcol),
                  pl.BlockSpec((A_HEADS, A_BLOCK, 3 * A_BLOCK), lambda n: (0, 0, 0)),
                  pl.BlockSpec((A_HEADS, 1, LANES), lambda n: (0, 0, 0))],
        out_specs=pl.BlockSpec((A_BLOCK, A_Q), lambda n: (n, 0)),
        out_shape=jax.ShapeDtypeStruct((L, A_Q), BF16),
        compiler_params=_params("parallel"),
        name="win_attn",
    )(proj, proj, proj, proj, proj, proj, proj, bias, sink)


def _cumsum_rows(tri, g):
    g1 = g.astype(BF16)
    r1 = g - g1.astype(F32)
    g2 = r1.astype(BF16)
    g3 = (r1 - g2.astype(F32)).astype(BF16)
    return _dot(tri, g1) + _dot(tri, g2) + _dot(tri, g3)


def _hgrn_tables(reverse):
    C = HGRN_CHUNK
    row = lax.broadcasted_iota(jnp.int32, (C, C), 0)
    col = lax.broadcasted_iota(jnp.int32, (C, C), 1)
    tri = ((col >= row) if reverse else (col <= row)).astype(BF16)
    diff = (row ^ col).astype(F32)
    high = (lax.bitcast_convert_type(diff, jnp.int32) >> 23) - 127
    wrong_side = (row < col) if not reverse else (row > col)
    level = jnp.where(row == col, 0, jnp.where(wrong_side, -1, high + 1))
    return tri, level


def _hgrn_chunk(reverse, q, z, v, lb, st_prev, tri, level, later_small):
    C = HGRN_CHUNK
    log_lb = jnp.log(lb)
    log_sig = jnp.minimum(z, 0.0) - jnp.log(1.0 + jnp.exp(-jnp.abs(z)))
    c2 = jnp.log1p(-lb) + log_sig
    g = jnp.maximum(log_lb, c2) + jnp.log(1.0 + jnp.exp(-jnp.abs(log_lb - c2)))
    k = (1.0 - lb) * (1.0 / (1.0 + jnp.exp(z)))

    b = _cumsum_rows(tri, g) * LOG2E
    btot = b[0:1, :] if reverse else b[C - 1:C, :]

    terms = [_dot_nt(q.astype(BF16), k.astype(BF16))]
    seg = b
    for j in range(int(math.log2(C))):
        m = 1 << j
        if m < SUBLANES:
            later = later_small[j] != 0
            up = pltpu.roll(seg, m, axis=0)
            down = pltpu.roll(seg, C - m, axis=0)
            if reverse:
                ref = jnp.where(later, seg, down)
                seg = jnp.where(later, up, seg)
            else:
                ref = jnp.where(later, up, seg)
                seg = jnp.where(later, seg, down)
            x = jnp.exp2(-jnp.abs(b - ref))
        else:
            expo, nxt = [], []
            for p in range(C // (2 * m)):
                lo, hi = slice(2 * p * m, (2 * p + 1) * m), slice((2 * p + 1) * m, (2 * p + 2) * m)
                if reverse:
                    expo += [b[lo] - seg[hi], seg[hi] - b[hi]]
                    nxt += [seg[lo], seg[lo]]
                else:
                    expo += [seg[lo] - b[lo], b[hi] - seg[lo]]
                    nxt += [seg[hi], seg[hi]]
            x = jnp.exp2(jnp.concatenate(expo, axis=0))
            seg = jnp.concatenate(nxt, axis=0)
        terms.append(_dot_nt((q * x).astype(BF16), (k * x).astype(BF16)))
    att = jnp.zeros((C, C), F32)
    for i in reversed(range(len(terms))):
        att = jnp.where(level == i, terms[i], att)
    vb = v.astype(BF16)
    o = _dot(att.astype(BF16), vb)

    o = o + _dot_nt((q * jnp.exp2(b)).astype(BF16), st_prev.astype(BF16))
    kd = (k * jnp.exp2(btot - b)).astype(BF16)
    st_next = jnp.exp2(btot) * st_prev + _dot(v.T.astype(BF16), kd)
    return o, st_next


def _hgrn_body(nch, qf_ref, zf_ref, if_ref, qb_ref, zb_ref, ib_ref, lb_ref, of_ref, ob_ref,
               sf_scr, sb_scr):
    @pl.when(pl.program_id(1) == 0)
    def _():
        sf_scr[...] = jnp.zeros_like(sf_scr)
        sb_scr[...] = jnp.zeros_like(sb_scr)

    C = HGRN_CHUNK
    tables = [_hgrn_tables(False), _hgrn_tables(True)]
    rowv = lax.broadcasted_iota(jnp.int32, (C, LANES), 0)
    later_small = [(rowv >> j) & 1 for j in range(int(math.log2(SUBLANES)))]

    def one(reverse, c, hh, q_ref, z_ref, i_ref, o_ref, s_scr):
        ci = (nch - 1 - c) if reverse else c
        rows = pl.ds(pl.multiple_of(ci * C, C), C)
        cols = slice(hh * B_DK, (hh + 1) * B_DK)
        qr = q_ref[rows, cols]
        tri, level = tables[int(reverse)]
        o, s_next = _hgrn_chunk(reverse, qr * _sigmoid(qr), z_ref[rows, cols], i_ref[rows, cols],
                                lb_ref[int(reverse), hh], s_scr[hh], tri, level, later_small)
        s_scr[hh] = s_next
        o_ref[rows, cols] = o

    def step(c, carry):
        for hh in range(HGRN_HEADS_PER_STEP):
            one(False, c, hh, qf_ref, zf_ref, if_ref, of_ref, sf_scr)
            one(True, c, hh, qb_ref, zb_ref, ib_ref, ob_ref, sb_scr)
        return carry

    lax.fori_loop(0, nch, step, 0, unroll=True)


def _hgrn(proj, lb2):
    L = proj.shape[0]
    tl = min(512, L)
    nt = L // tl
    nch = tl // HGRN_CHUNK
    hps = HGRN_HEADS_PER_STEP
    width = hps * B_DK
    lb4 = lb2.reshape(2, B_HEADS, 1, B_DK)
    col = lambda part, h: (B_COL0 + part * B_W) // width + h
    fwd = lambda part: pl.BlockSpec((tl, width), lambda h, t: (t, col(part, h)))
    bwd = lambda part: pl.BlockSpec((tl, width), lambda h, t: (nt - 1 - t, col(part, h)))
    state = pltpu.VMEM((hps, B_DV, B_DK), F32)
    return pl.pallas_call(
        functools.partial(_hgrn_body, nch),
        grid=(B_HEADS // hps, nt),
        in_specs=[fwd(0), fwd(1), fwd(3), bwd(0), bwd(2), bwd(3),
                  pl.BlockSpec((2, hps, 1, B_DK), lambda h, t: (0, h, 0, 0))],
        out_specs=[pl.BlockSpec((tl, width), lambda h, t: (t, h)),
                   pl.BlockSpec((tl, width), lambda h, t: (nt - 1 - t, h))],
        out_shape=[jax.ShapeDtypeStruct((L, B_VW), F32), jax.ShapeDtypeStruct((L, B_VW), F32)],
        scratch_shapes=[state, state],
        compiler_params=_params("parallel", "arbitrary"),
        name="hgrn",
    )(proj, proj, proj, proj, proj, proj, lb4)


def _rope_prep_body(x_ref, qw_ref, kw_ref, cos_ref, sin_ref, q_ref, kt_ref, v_ref):
    cos = cos_ref[...]
    sin = sin_ref[...]
    lane = lax.broadcasted_iota(jnp.int32, cos.shape, 1)
    even = (lane % 2) == 0
    low = lane < HEAD_DIM
    ones_lane = jnp.where(lane == HEAD_DIM, 1.0, 0.0)
    tile = lambda i: slice(i * LANES, (i + 1) * LANES)
    halves = lambda y: (y, pltpu.roll(y, HEAD_DIM, axis=1))

    def norm_rope(x, w):
        sq = x * x
        ms = jnp.where(low, jnp.sum(jnp.where(low, sq, 0.0), axis=-1, keepdims=True),
                       jnp.sum(jnp.where(low, 0.0, sq), axis=-1, keepdims=True)) * (1.0 / HEAD_DIM)
        y = x * lax.rsqrt(ms + EPS) * w
        partner = jnp.where(even, pltpu.roll(y, LANES - 1, axis=1), pltpu.roll(y, 1, axis=1))
        return y * cos + partner * sin

    for t in range(C_Q // LANES):
        y = norm_rope(x_ref[:, tile(t)], qw_ref[...]) * (LOG2E * HEAD_DIM ** -0.5)
        for half, yh in enumerate(halves(y)):
            q_ref[:, tile(2 * t + half)] = yh.astype(q_ref.dtype)
    for t in range(C_KV // LANES):
        y = norm_rope(x_ref[:, tile(C_Q // LANES + t)], kw_ref[...])
        v = x_ref[:, tile((C_Q + C_KV) // LANES + t)]
        for half, (yh, vh) in enumerate(zip(halves(y), halves(v))):
            kt_ref[tile(2 * t + half), :] = jnp.where(low, yh, 0.0).T.astype(kt_ref.dtype)
            v_ref[:, tile(2 * t + half)] = jnp.where(low, vh, ones_lane).astype(v_ref.dtype)


def _rope_prep(proj, qw, kw, cos_t, sin_t):
    L = proj.shape[0]
    tl = min(256, L)
    width = C_Q + 2 * C_KV
    vec = pl.BlockSpec((1, LANES), lambda i: (0, 0))
    tab = pl.BlockSpec((tl, LANES), lambda i: (i, 0))
    return pl.pallas_call(
        _rope_prep_body,
        grid=(L // tl,),
        in_specs=[pl.BlockSpec((tl, width), lambda i: (i, proj.shape[1] // width - 1)),
                  vec, vec, tab, tab],
        out_specs=[pl.BlockSpec((tl, Q_PAD), lambda i: (i, 0)),
                   pl.BlockSpec((KV_PAD, tl), lambda i: (0, i)),
                   pl.BlockSpec((tl, KV_PAD), lambda i: (i, 0))],
        out_shape=[jax.ShapeDtypeStruct((L, Q_PAD), BF16),
                   jax.ShapeDtypeStruct((KV_PAD, L), BF16),
                   jax.ShapeDtypeStruct((L, KV_PAD), BF16)],
        compiler_params=_params("parallel"),
        name="rope_prep",
    )(proj, qw, kw, cos_t, sin_t)


def _axial_attn_body(tq, tk, nk, q_ref, kt_ref, v_ref, o_ref, m_scr, acc_scr):
    q = jnp.concatenate([q_ref[:, h * LANES:(h + 1) * LANES] for h in range(GROUP)], axis=0)
    m_scr[...] = jnp.full(m_scr.shape, -jnp.inf, F32)
    acc_scr[...] = jnp.zeros(acc_scr.shape, F32)

    def step(t, carry):
        k0 = pl.multiple_of(t * tk, tk)
        s = _dot(q, kt_ref[:, pl.ds(k0, tk)])
        parts = [s[:, i * LANES:(i + 1) * LANES] for i in range(tk // LANES)]
        m = m_scr[...]
        m_new = jnp.maximum(m, jnp.max(functools.reduce(jnp.maximum, parts), axis=-1,
                                       keepdims=True))
        p = jnp.concatenate([jnp.exp2(part - m_new) for part in parts], axis=1)
        acc_scr[...] = jnp.exp2(m - m_new) * acc_scr[...] + _dot(p.astype(BF16),
                                                                  v_ref[pl.ds(k0, tk), :])
        m_scr[...] = m_new
        return carry

    lax.fori_loop(0, nk, step, 0, unroll=AXIAL_UNROLL)
    acc = acc_scr[...]
    lane = lax.broadcasted_iota(jnp.int32, acc.shape, 1)
    o = jnp.where(lane < HEAD_DIM, acc / acc[:, HEAD_DIM:HEAD_DIM + 1], 0.0)
    o0, o1, o2 = o[0:tq], o[tq:2 * tq], o[2 * tq:3 * tq]
    o_ref[:, 0:LANES] = (o0 + pltpu.roll(o1, HEAD_DIM, axis=1)).astype(o_ref.dtype)
    o_ref[:, LANES:2 * LANES] = o2.astype(o_ref.dtype)


def _axial_attn(q, kt, v):
    L = q.shape[0]
    tq = min(512, L)
    tk = min(512, L)
    return pl.pallas_call(
        functools.partial(_axial_attn_body, tq, tk, L // tk),
        grid=(C_KV_HEADS, L // tq),
        in_specs=[pl.BlockSpec((tq, GROUP * LANES), lambda g, i: (i, g)),
                  pl.BlockSpec((LANES, L), lambda g, i: (g, 0)),
                  pl.BlockSpec((L, LANES), lambda g, i: (0, g))],
        out_specs=pl.BlockSpec((tq, 2 * LANES), lambda g, i: (i, g)),
        out_shape=jax.ShapeDtypeStruct((L, C_KV_HEADS * 2 * LANES), BF16),
        scratch_shapes=[pltpu.VMEM((GROUP * tq, LANES), F32), pltpu.VMEM((GROUP * tq, LANES), F32)],
        compiler_params=_params("parallel", "arbitrary"),
        name="axial_attn",
    )(q, kt, v)


def _out_proj_body(x_ref, ya_ref, of_ref, ob_ref, g0_ref, g1_ref, yc_ref, wa_ref, wb_ref, wc_ref,
                   gw_ref, nw_ref, o_ref):
    tot = of_ref[...] + ob_ref[...]
    normed = jnp.concatenate([_rms(tot[:, h * B_DV:(h + 1) * B_DV], gw_ref[...])
                              for h in range(B_HEADS)], axis=1)
    g = jnp.concatenate([g0_ref[...], g1_ref[...]], axis=1)
    yb = (normed * (g * _sigmoid(g))).astype(BF16)
    y = _dot(ya_ref[...], wa_ref[...]) + _dot(yb, wb_ref[...]) + _dot(yc_ref[...], wc_ref[...])
    o_ref[...] = x_ref[...] + _rms(y, nw_ref[...])


def _out_proj(x, ya, o_fwd, o_bwd, proj, yc, wa, wb, wc, gw, nw):
    L = x.shape[0]
    tm = min(512, L)
    row = lambda w: pl.BlockSpec((tm, w), lambda i: (i, 0))
    full = lambda a: pl.BlockSpec(a.shape, lambda i: (0, 0))
    g_col0 = B_COL0 + 3 * B_W + B_VW
    gblk = lambda k: pl.BlockSpec((tm, B_VW // 2), lambda i: (i, g_col0 // (B_VW // 2) + k))
    return pl.pallas_call(
        _out_proj_body,
        grid=(L // tm,),
        in_specs=[row(D_MODEL), row(ya.shape[1]), row(B_VW), row(B_VW), gblk(0), gblk(1),
                  row(yc.shape[1]), full(wa), full(wb), full(wc), full(gw),
                  pl.BlockSpec((1, D_MODEL), lambda i: (0, 0))],
        out_specs=row(D_MODEL),
        out_shape=jax.ShapeDtypeStruct((L, D_MODEL), F32),
        compiler_params=_params("parallel"),
        name="out_proj",
    )(x, ya, o_fwd, o_bwd, proj, proj, yc, wa, wb, wc, gw, nw)


def _split_w_out(w_out):
    wa = w_out[:A_Q]
    wb = w_out[A_Q:A_Q + B_VW]
    wc = w_out[A_Q + B_VW:].reshape(C_KV_HEADS, GROUP * HEAD_DIM, D_MODEL)
    wc = jnp.pad(wc, ((0, 0), (0, 2 * LANES - GROUP * HEAD_DIM), (0, 0)))
    return wa.astype(BF16), wb.astype(BF16), wc.reshape(C_KV_HEADS * 2 * LANES, D_MODEL).astype(BF16)


def _t5_bucket(rel):
    nb = REL_BUCKETS // 2
    max_exact = nb // 2
    n = jnp.abs(rel)
    nf = jnp.maximum(n, 1).astype(F32)
    large = max_exact + (jnp.log(nf / max_exact) / math.log(REL_MAX_DIST / max_exact)
                         * (nb - max_exact)).astype(jnp.int32)
    large = jnp.minimum(large, nb - 1)
    return jnp.where(rel > 0, nb, 0) + jnp.where(n < max_exact, n, large)


def _rope_tables(L):
    half = HEAD_DIM // 2
    inv = 1.0 / (ROPE_THETA ** (jnp.arange(0, half, 2, dtype=F32) / half))
    pos = jnp.arange(L)
    ang = jnp.concatenate([(pos // GRID_W).astype(F32)[:, None] * inv,
                           (pos % GRID_W).astype(F32)[:, None] * inv], axis=-1)
    cos = jnp.repeat(jnp.cos(ang), 2, axis=-1)
    sin = jnp.repeat(jnp.sin(ang), 2, axis=-1) * jnp.tile(jnp.array([-1.0, 1.0], F32), half)
    return jnp.tile(cos, (1, LANES // HEAD_DIM)), jnp.tile(sin, (1, LANES // HEAD_DIM))


def _band_bias(rel_bias):
    period = 4 * A_BLOCK
    rel = jnp.arange(period) - A_BLOCK
    vals = rel_bias.astype(F32)[_t5_bucket(rel)] * LOG2E
    vals = jnp.where((jnp.abs(rel) <= WINDOW)[:, None], vals, NEG).T
    flat = jnp.tile(vals, (1, A_BLOCK))[:, :A_BLOCK * (period - 1)]
    return flat.reshape(A_HEADS, A_BLOCK, period - 1)[:, :, :3 * A_BLOCK]


def kernel(x, w_in, w_out, ffn1_gate, ffn1_up, ffn1_down, ffn2_gate, ffn2_up, ffn2_down,
           norm_w, sink_logits, qk_norm_w, hgrn_lb, hgrn_norm_w, rel_bias):
    B_, L, _ = x.shape
    depth = w_in.shape[0]
    bias = _band_bias(rel_bias)
    cos_t, sin_t = _rope_tables(L)
    lb_c = jnp.cumsum(jax.nn.softmax(hgrn_lb.astype(F32), axis=0), axis=0)
    lbs = lb_c - lb_c[0:1]
    qkw = jnp.tile(qk_norm_w.astype(F32), (1, 1, LANES // HEAD_DIM))
    nw = norm_w.astype(F32).reshape(depth, 6, 1, D_MODEL)
    ffn1_b = (ffn1_gate, ffn1_up, ffn1_down)
    ffn2_b = (ffn2_gate, ffn2_up, ffn2_down)

    outs = []
    for b in range(B_):
        xb = x[b]
        for l in range(depth):
            xb = _ffn(xb, nw[l, 0], nw[l, 1], *ffn1_b, l)
            proj = _in_proj(xb, nw[l, 2], w_in, l)
            ya = _win_attn(proj, bias,
                           jnp.broadcast_to((sink_logits[l].astype(F32) * LOG2E)
                                            .reshape(A_HEADS, 1, 1), (A_HEADS, 1, LANES)))
            o_fwd, o_bwd = _hgrn(proj, lbs[l])
            q, kt, v = _rope_prep(proj, qkw[l, 0:1], qkw[l, 1:2], cos_t, sin_t)
            yc = _axial_attn(q, kt, v)
            oa, ob, oc = _split_w_out(w_out[l])
            xb = _out_proj(xb, ya, o_fwd, o_bwd, proj, yc, oa, ob, oc,
                           hgrn_norm_w[l].astype(F32).reshape(1, B_DV), nw[l, 3])
            xb = _ffn(xb, nw[l, 4], nw[l, 5], *ffn2_b, l)
        outs.append(xb)
    return jnp.stack(outs, axis=0)
```

```python
import functools
import math

import jax
import jax.numpy as jnp
from jax import lax
from jax.experimental import pallas as pl
from jax.experimental.pallas import tpu as pltpu

D_MODEL = 2048
HEAD_DIM = 64
A_HEADS = 12
A_KV_HEADS = 4
WINDOW = 128
A_BLOCK = 128
B_HEADS = 4
B_DK = 128
B_DV = 128
C_HEADS = 12
C_KV_HEADS = 4
ROPE_THETA = 10000.0
GRID_W = 64
REL_BUCKETS = 32
REL_MAX_DIST = 128
D_FF = 5632
EPS = 1e-6
NEG = -1e30

A_Q = A_HEADS * HEAD_DIM
A_KV = A_KV_HEADS * HEAD_DIM
B_W = B_HEADS * B_DK
B_VW = B_HEADS * B_DV
C_Q = C_HEADS * HEAD_DIM
C_KV = C_KV_HEADS * HEAD_DIM
B_COL0 = A_Q + 2 * A_KV

LANES = 128
SUBLANES = 8
ROW_CHUNK = 2 * SUBLANES
Q_PAD = A_HEADS * LANES
KV_PAD = A_KV_HEADS * LANES
GROUP = A_HEADS // A_KV_HEADS
VMEM_LIMIT = 52 * 1024 * 1024
FFN_ROWS = 1024
FFN_COLS = 256
FFN_VMEM_LIMIT = 58 * 1024 * 1024

LOG2E = math.log2(math.e)
AXIAL_UNROLL = 8

WIN_QBLOCKS = 2
HGRN_CHUNK = 128
HGRN_HEADS_PER_STEP = 2

BF16 = jnp.bfloat16
F32 = jnp.float32


def _params(*sem):
    return pltpu.CompilerParams(dimension_semantics=sem, vmem_limit_bytes=VMEM_LIMIT)


def _dot(a, b):
    return jnp.dot(a, b, preferred_element_type=F32)


def _dot_nt(a, b):
    return lax.dot_general(a, b, (((1,), (1,)), ((), ())), preferred_element_type=F32)


def _rms(x, w):
    sq = x * x
    width = x.shape[-1]
    folded = functools.reduce(
        jnp.add, [sq[:, i * LANES:(i + 1) * LANES] for i in range(width // LANES)])
    ms = jnp.sum(folded, axis=-1, keepdims=True) * (1.0 / width)
    return x * lax.rsqrt(ms + EPS) * w


def _for_row_chunks(nrows, fn):
    for start in range(0, nrows, ROW_CHUNK):
        fn(slice(start, start + ROW_CHUNK))


def _sigmoid(x):
    return 1.0 / (1.0 + jnp.exp(-x))


def _ffn_body(nf, x_ref, prew_ref, postw_ref, wg_ref, wu_ref, wd_ref, o_ref, h_scr):
    j = pl.program_id(1)

    nrows = x_ref.shape[0]

    @pl.when(j == 0)
    def _():
        def pre(rows):
            h_scr[rows, :] = _rms(x_ref[rows, :], prew_ref[...]).astype(BF16)
            o_ref[rows, :] = jnp.zeros((ROW_CHUNK, D_MODEL), F32)

        _for_row_chunks(nrows, pre)

    h = h_scr[...]
    g = _dot(h, wg_ref[...].astype(BF16))
    u = _dot(h, wu_ref[...].astype(BF16))
    a = (g * _sigmoid(g)) * u
    o_ref[...] += _dot(a.astype(BF16), wd_ref[...].astype(BF16))

    @pl.when(j == nf - 1)
    def _():
        def post(rows):
            o_ref[rows, :] = x_ref[rows, :] + 0.5 * _rms(o_ref[rows, :], postw_ref[...])

        _for_row_chunks(nrows, post)


def _ffn(x, prew, postw, wg, wu, wd, layer):
    L = x.shape[0]
    tm = min(FFN_ROWS, L)
    tf = FFN_COLS
    nf = D_FF // tf
    row = pl.BlockSpec((tm, D_MODEL), lambda i, j: (i, 0))
    vec = pl.BlockSpec((1, D_MODEL), lambda i, j: (0, 0))
    return pl.pallas_call(
        functools.partial(_ffn_body, nf),
        grid=(L // tm, nf),
        in_specs=[row, vec, vec,
                  pl.BlockSpec((None, D_MODEL, tf), lambda i, j: (layer, 0, j)),
                  pl.BlockSpec((None, D_MODEL, tf), lambda i, j: (layer, 0, j)),
                  pl.BlockSpec((None, tf, D_MODEL), lambda i, j: (layer, j, 0))],
        out_specs=row,
        out_shape=jax.ShapeDtypeStruct((L, D_MODEL), F32),
        scratch_shapes=[pltpu.VMEM((tm, D_MODEL), BF16)],
        compiler_params=pltpu.CompilerParams(dimension_semantics=("parallel", "arbitrary"),
                                             vmem_limit_bytes=FFN_VMEM_LIMIT),
        name="ffn",
    )(x, prew, postw, wg, wu, wd)


def _in_proj_body(x_ref, nw_ref, w_ref, o_ref, h_scr):
    @pl.when(pl.program_id(1) == 0)
    def _():
        def pre(rows):
            h_scr[rows, :] = _rms(x_ref[rows, :], nw_ref[...]).astype(BF16)

        _for_row_chunks(x_ref.shape[0], pre)

    o_ref[...] = _dot(h_scr[...], w_ref[...].astype(BF16))


def _in_proj(x, nw, w, layer):
    L = x.shape[0]
    N = w.shape[2]
    tm = min(1024, L)
    tn = 512
    return pl.pallas_call(
        _in_proj_body,
        grid=(L // tm, N // tn),
        in_specs=[pl.BlockSpec((tm, D_MODEL), lambda i, j: (i, 0)),
                  pl.BlockSpec((1, D_MODEL), lambda i, j: (0, 0)),
                  pl.BlockSpec((None, D_MODEL, tn), lambda i, j: (layer, 0, j))],
        out_specs=pl.BlockSpec((tm, tn), lambda i, j: (i, j)),
        out_shape=jax.ShapeDtypeStruct((L, N), F32),
        scratch_shapes=[pltpu.VMEM((tm, D_MODEL), BF16)],
        compiler_params=_params("parallel", "arbitrary"),
        name="in_proj",
    )(x, nw, w)


def _win_attn_body(nb, q_ref, *refs):
    nkb = WIN_QBLOCKS + 2
    k_refs, v_refs = refs[:nkb], refs[nkb:2 * nkb]
    bias_ref, sink_ref, o_ref = refs[2 * nkb:]
    k_blocks = [r[...].astype(BF16) for r in k_refs]
    v_blocks = [r[...].astype(BF16) for r in v_refs]
    for b in range(WIN_QBLOCKS):
        rows = slice(b * A_BLOCK, (b + 1) * A_BLOCK)
        _win_attn_block(pl.program_id(0) * WIN_QBLOCKS + b, nb, q_ref.at[rows, :],
                        jnp.concatenate(k_blocks[b:b + 3], axis=0),
                        jnp.concatenate(v_blocks[b:b + 3], axis=0),
                        bias_ref, sink_ref, o_ref.at[rows, :])


def _win_attn_block(n, nb, q_ref, k_all, v_all, bias_ref, sink_ref, o_ref):
    edge_lo = jnp.where(n == 0, NEG, 0.0)
    edge_hi = jnp.where(n == nb - 1, NEG, 0.0)
    low = lax.broadcasted_iota(jnp.int32, (A_BLOCK, LANES), 1) < HEAD_DIM
    tile = lambda i: slice(i * LANES, (i + 1) * LANES)
    outs = []
    for g in range(A_KV_HEADS):
        heads = range(g * GROUP, (g + 1) * GROUP)
        in_kv_half = low if g % 2 == 0 else jnp.logical_not(low)
        qs = []
        for h in heads:
            qh = q_ref[:, tile(h // 2)] * (LOG2E * HEAD_DIM ** -0.5)
            if h % 2 != g % 2:
                qh = pltpu.roll(qh, HEAD_DIM, axis=1)
            qs.append(jnp.where(in_kv_half, qh, 0.0).astype(BF16))
        s3 = _dot_nt(jnp.concatenate(qs, axis=0), k_all[:, tile(g // 2)])
        ps, dens = [], []
        for j, h in enumerate(heads):
            s = s3[j * A_BLOCK:(j + 1) * A_BLOCK] + bias_ref[h]
            t = [s[:, 0:LANES] + edge_lo, s[:, LANES:2 * LANES], s[:, 2 * LANES:] + edge_hi]
            sink = sink_ref[h]
            m = jnp.maximum(jnp.max(jnp.maximum(jnp.maximum(t[0], t[1]), t[2]), axis=-1,
                                    keepdims=True), sink)
            p = [jnp.exp2(ti - m) for ti in t]
            dens.append(jnp.sum(p[0] + p[1] + p[2], axis=-1, keepdims=True) + jnp.exp2(sink - m))
            ps.append(jnp.concatenate(p, axis=1).astype(BF16))
        o3 = _dot(jnp.concatenate(ps, axis=0), v_all[:, tile(g // 2)])
        for j, h in enumerate(heads):
            o = o3[j * A_BLOCK:(j + 1) * A_BLOCK] / dens[j]
            outs.append(o if h % 2 == g % 2 else pltpu.roll(o, HEAD_DIM, axis=1))
    for pr in range(A_HEADS // 2):
        o_ref[:, tile(pr)] = jnp.where(low, outs[2 * pr], outs[2 * pr + 1]).astype(o_ref.dtype)


def _win_attn(proj, bias, sink):
    L = proj.shape[0]
    nb = L // A_BLOCK
    qb = WIN_QBLOCKS
    kcol, vcol = A_Q // A_KV, A_Q // A_KV + 1
    kblk = lambda r, c: pl.BlockSpec(
        (A_BLOCK, A_KV), lambda n: (jnp.clip(n * qb - 1 + r, 0, nb - 1), c))
    kv_specs = [kblk(r, c) for c in (kcol, vcol) for r in range(qb + 2)]
    return pl.pallas_call(
        functools.partial(_win_attn_body, nb),
        grid=(nb // qb,),
        in_specs=[pl.BlockSpec((qb * A_BLOCK, A_Q), lambda n: (n, 0))] + kv_specs + [
            pl.BlockSpec((A_HEADS, A_BLOCK, 3 * A_BLOCK), lambda n: (0, 0, 0)),
            pl.BlockSpec((A_HEADS, 1, LANES), lambda n: (0, 0, 0))],
        out_specs=pl.BlockSpec((qb * A_BLOCK, A_Q), lambda n: (n, 0)),
        out_shape=jax.ShapeDtypeStruct((L, A_Q), BF16),
        compiler_params=_params("parallel"),
        name="win_attn",
    )(proj, *([proj] * len(kv_specs)), bias, sink)


def _cumsum_rows(tri, g):
    g1 = g.astype(BF16)
    r1 = g - g1.astype(F32)
    g2 = r1.astype(BF16)
    g3 = (r1 - g2.astype(F32)).astype(BF16)
    return _dot(tri, g1) + _dot(tri, g2) + _dot(tri, g3)


def _hgrn_tables(reverse):
    C = HGRN_CHUNK
    row = lax.broadcasted_iota(jnp.int32, (C, C), 0)
    col = lax.broadcasted_iota(jnp.int32, (C, C), 1)
    tri = ((col >= row) if reverse else (col <= row)).astype(BF16)
    diff = (row ^ col).astype(F32)
    high = (lax.bitcast_convert_type(diff, jnp.int32) >> 23) - 127
    wrong_side = (row < col) if not reverse else (row > col)
    level = jnp.where(row == col, 0, jnp.where(wrong_side, -1, high + 1))
    return tri, level


def _hgrn_chunk(reverse, q, z, v, lb, st_prev, tri, level, later_small):
    C = HGRN_CHUNK
    log_lb = jnp.log(lb)
    log_sig = jnp.minimum(z, 0.0) - jnp.log(1.0 + jnp.exp(-jnp.abs(z)))
    c2 = jnp.log1p(-lb) + log_sig
    g = jnp.maximum(log_lb, c2) + jnp.log(1.0 + jnp.exp(-jnp.abs(log_lb - c2)))
    k = (1.0 - lb) * (1.0 / (1.0 + jnp.exp(z)))

    b = _cumsum_rows(tri, g) * LOG2E
    btot = b[0:1, :] if reverse else b[C - 1:C, :]

    terms = [_dot_nt(q.astype(BF16), k.astype(BF16))]
    seg = b
    for j in range(int(math.log2(C))):
        m = 1 << j
        if m < SUBLANES:
            later = later_small[j] != 0
            up = pltpu.roll(seg, m, axis=0)
            down = pltpu.roll(seg, C - m, axis=0)
            if reverse:
                ref = jnp.where(later, seg, down)
                seg = jnp.where(later, up, seg)
            else:
                ref = jnp.where(later, up, seg)
                seg = jnp.where(later, seg, down)
            x = jnp.exp2(-jnp.abs(b - ref))
        else:
            expo, nxt = [], []
            for p in range(C // (2 * m)):
                lo, hi = slice(2 * p * m, (2 * p + 1) * m), slice((2 * p + 1) * m, (2 * p + 2) * m)
                if reverse:
                    expo += [b[lo] - seg[hi], seg[hi] - b[hi]]
                    nxt += [seg[lo], seg[lo]]
                else:
                    expo += [seg[lo] - b[lo], b[hi] - seg[lo]]
                    nxt += [seg[hi], seg[hi]]
            x = jnp.exp2(jnp.concatenate(expo, axis=0))
            seg = jnp.concatenate(nxt, axis=0)
        terms.append(_dot_nt((q * x).astype(BF16), (k * x).astype(BF16)))
    att = jnp.zeros((C, C), F32)
    for i in reversed(range(len(terms))):
        att = jnp.where(level == i, terms[i], att)
    vb = v.astype(BF16)
    o = _dot(att.astype(BF16), vb)

    o = o + _dot_nt((q * jnp.exp2(b)).astype(BF16), st_prev.astype(BF16))
    kd = (k * jnp.exp2(btot - b)).astype(BF16)
    st_next = jnp.exp2(btot) * st_prev + _dot(v.T.astype(BF16), kd)
    return o, st_next


def _hgrn_body(nch, qf_ref, zf_ref, if_ref, qb_ref, zb_ref, ib_ref, lb_ref, of_ref, ob_ref,
               sf_scr, sb_scr):
    @pl.when(pl.program_id(1) == 0)
    def _():
        sf_scr[...] = jnp.zeros_like(sf_scr)
        sb_scr[...] = jnp.zeros_like(sb_scr)

    C = HGRN_CHUNK
    tables = [_hgrn_tables(False), _hgrn_tables(True)]
    rowv = lax.broadcasted_iota(jnp.int32, (C, LANES), 0)
    later_small = [(rowv >> j) & 1 for j in range(int(math.log2(SUBLANES)))]

    def one(reverse, c, hh, q_ref, z_ref, i_ref, o_ref, s_scr):
        ci = (nch - 1 - c) if reverse else c
        rows = pl.ds(pl.multiple_of(ci * C, C), C)
        cols = slice(hh * B_DK, (hh + 1) * B_DK)
        qr = q_ref[rows, cols]
        tri, level = tables[int(reverse)]
        o, s_next = _hgrn_chunk(reverse, qr * _sigmoid(qr), z_ref[rows, cols], i_ref[rows, cols],
                                lb_ref[int(reverse), hh], s_scr[hh], tri, level, later_small)
        s_scr[hh] = s_next
        o_ref[rows, cols] = o

    def step(c, carry):
        for hh in range(HGRN_HEADS_PER_STEP):
            one(False, c, hh, qf_ref, zf_ref, if_ref, of_ref, sf_scr)
            one(True, c, hh, qb_ref, zb_ref, ib_ref, ob_ref, sb_scr)
        return carry

    lax.fori_loop(0, nch, step, 0, unroll=True)


def _hgrn(proj, lb2):
    L = proj.shape[0]
    tl = min(512, L)
    nt = L // tl
    nch = tl // HGRN_CHUNK
    hps = HGRN_HEADS_PER_STEP
    width = hps * B_DK
    lb4 = lb2.reshape(2, B_HEADS, 1, B_DK)
    col = lambda part, h: (B_COL0 + part * B_W) // width + h
    fwd = lambda part: pl.BlockSpec((tl, width), lambda h, t: (t, col(part, h)))
    bwd = lambda part: pl.BlockSpec((tl, width), lambda h, t: (nt - 1 - t, col(part, h)))
    state = pltpu.VMEM((hps, B_DV, B_DK), F32)
    return pl.pallas_call(
        functools.partial(_hgrn_body, nch),
        grid=(B_HEADS // hps, nt),
        in_specs=[fwd(0), fwd(1), fwd(3), bwd(0), bwd(2), bwd(3),
                  pl.BlockSpec((2, hps, 1, B_DK), lambda h, t: (0, h, 0, 0))],
        out_specs=[pl.BlockSpec((tl, width), lambda h, t: (t, h)),
                   pl.BlockSpec((tl, width), lambda h, t: (nt - 1 - t, h))],
        out_shape=[jax.ShapeDtypeStruct((L, B_VW), F32), jax.ShapeDtypeStruct((L, B_VW), F32)],
        scratch_shapes=[state, state],
        compiler_params=_params("parallel", "arbitrary"),
        name="hgrn",
    )(proj, proj, proj, proj, proj, proj, lb4)


def _rope_prep_body(x_ref, qw_ref, kw_ref, cos_ref, sin_ref, q_ref, kt_ref, v_ref):
    cos = cos_ref[...]
    sin = sin_ref[...]
    lane = lax.broadcasted_iota(jnp.int32, cos.shape, 1)
    even = (lane % 2) == 0
    low = lane < HEAD_DIM
    ones_lane = jnp.where(lane == HEAD_DIM, 1.0, 0.0)
    tile = lambda i: slice(i * LANES, (i + 1) * LANES)
    halves = lambda y: (y, pltpu.roll(y, HEAD_DIM, axis=1))

    def norm_rope(x, w):
        sq = x * x
        ms = jnp.where(low, jnp.sum(jnp.where(low, sq, 0.0), axis=-1, keepdims=True),
                       jnp.sum(jnp.where(low, 0.0, sq), axis=-1, keepdims=True)) * (1.0 / HEAD_DIM)
        y = x * lax.rsqrt(ms + EPS) * w
        partner = jnp.where(even, pltpu.roll(y, LANES - 1, axis=1), pltpu.roll(y, 1, axis=1))
        return y * cos + partner * sin

    for t in range(C_Q // LANES):
        y = norm_rope(x_ref[:, tile(t)], qw_ref[...]) * (LOG2E * HEAD_DIM ** -0.5)
        for half, yh in enumerate(halves(y)):
            q_ref[:, tile(2 * t + half)] = yh.astype(q_ref.dtype)
    for t in range(C_KV // LANES):
        y = norm_rope(x_ref[:, tile(C_Q // LANES + t)], kw_ref[...])
        v = x_ref[:, tile((C_Q + C_KV) // LANES + t)]
        for half, (yh, vh) in enumerate(zip(halves(y), halves(v))):
            kt_ref[tile(2 * t + half), :] = jnp.where(low, yh, 0.0).T.astype(kt_ref.dtype)
            v_ref[:, tile(2 * t + half)] = jnp.where(low, vh, ones_lane).astype(v_ref.dtype)


def _rope_prep(proj, qw, kw, cos_t, sin_t):
    L = proj.shape[0]
    tl = min(256, L)
    width = C_Q + 2 * C_KV
    vec = pl.BlockSpec((1, LANES), lambda i: (0, 0))
    tab = pl.BlockSpec((tl, LANES), lambda i: (i, 0))
    return pl.pallas_call(
        _rope_prep_body,
        grid=(L // tl,),
        in_specs=[pl.BlockSpec((tl, width), lambda i: (i, proj.shape[1] // width - 1)),
                  vec, vec, tab, tab],
        out_specs=[pl.BlockSpec((tl, Q_PAD), lambda i: (i, 0)),
                   pl.BlockSpec((KV_PAD, tl), lambda i: (0, i)),
                   pl.BlockSpec((tl, KV_PAD), lambda i: (i, 0))],
        out_shape=[jax.ShapeDtypeStruct((L, Q_PAD), BF16),
                   jax.ShapeDtypeStruct((KV_PAD, L), BF16),
                   jax.ShapeDtypeStruct((L, KV_PAD), BF16)],
        compiler_params=_params("parallel"),
        name="rope_prep",
    )(proj, qw, kw, cos_t, sin_t)


def _axial_attn_body(tq, tk, nk, q_ref, kt_ref, v_ref, o_ref, m_scr, acc_scr):
    q = jnp.concatenate([q_ref[:, h * LANES:(h + 1) * LANES] for h in range(GROUP)], axis=0)
    m_scr[...] = jnp.full(m_scr.shape, -jnp.inf, F32)
    acc_scr[...] = jnp.zeros(acc_scr.shape, F32)

    def step(t, carry):
        k0 = pl.multiple_of(t * tk, tk)
        s = _dot(q, kt_ref[:, pl.ds(k0, tk)])
        parts = [s[:, i * LANES:(i + 1) * LANES] for i in range(tk // LANES)]
        m = m_scr[...]
        m_new = jnp.maximum(m, jnp.max(functools.reduce(jnp.maximum, parts), axis=-1,
                                       keepdims=True))
        p = jnp.concatenate([jnp.exp2(part - m_new) for part in parts], axis=1)
        acc_scr[...] = jnp.exp2(m - m_new) * acc_scr[...] + _dot(p.astype(BF16),
                                                                  v_ref[pl.ds(k0, tk), :])
        m_scr[...] = m_new
        return carry

    lax.fori_loop(0, nk, step, 0, unroll=AXIAL_UNROLL)
    acc = acc_scr[...]
    lane = lax.broadcasted_iota(jnp.int32, acc.shape, 1)
    o = jnp.where(lane < HEAD_DIM, acc / acc[:, HEAD_DIM:HEAD_DIM + 1], 0.0)
    o0, o1, o2 = o[0:tq], o[tq:2 * tq], o[2 * tq:3 * tq]
    o_ref[:, 0:LANES] = (o0 + pltpu.roll(o1, HEAD_DIM, axis=1)).astype(o_ref.dtype)
    o_ref[:, LANES:2 * LANES] = o2.astype(o_ref.dtype)


def _axial_attn(q, kt, v):
    L = q.shape[0]
    tq = min(1024, L)
    tk = min(512, L)
    return pl.pallas_call(
        functools.partial(_axial_attn_body, tq, tk, L // tk),
        grid=(C_KV_HEADS, L // tq),
        in_specs=[pl.BlockSpec((tq, GROUP * LANES), lambda g, i: (i, g)),
                  pl.BlockSpec((LANES, L), lambda g, i: (g, 0)),
                  pl.BlockSpec((L, LANES), lambda g, i: (0, g))],
        out_specs=pl.BlockSpec((tq, 2 * LANES), lambda g, i: (i, g)),
        out_shape=jax.ShapeDtypeStruct((L, C_KV_HEADS * 2 * LANES), BF16),
        scratch_shapes=[pltpu.VMEM((GROUP * tq, LANES), F32), pltpu.VMEM((GROUP * tq, LANES), F32)],
        compiler_params=_params("parallel", "arbitrary"),
        name="axial_attn",
    )(q, kt, v)


def _out_proj_body(x_ref, ya_ref, of_ref, ob_ref, g0_ref, g1_ref, yc_ref, wa_ref, wb_ref, wc_ref,
                   gw_ref, nw_ref, o_ref):
    tot = of_ref[...] + ob_ref[...]
    normed = jnp.concatenate([_rms(tot[:, h * B_DV:(h + 1) * B_DV], gw_ref[...])
                              for h in range(B_HEADS)], axis=1)
    g = jnp.concatenate([g0_ref[...], g1_ref[...]], axis=1)
    yb = (normed * (g * _sigmoid(g))).astype(BF16)
    y = _dot(ya_ref[...], wa_ref[...]) + _dot(yb, wb_ref[...]) + _dot(yc_ref[...], wc_ref[...])
    o_ref[...] = x_ref[...] + _rms(y, nw_ref[...])


def _out_proj(x, ya, o_fwd, o_bwd, proj, yc, wa, wb, wc, gw, nw):
    L = x.shape[0]
    tm = min(512, L)
    row = lambda w: pl.BlockSpec((tm, w), lambda i: (i, 0))
    full = lambda a: pl.BlockSpec(a.shape, lambda i: (0, 0))
    g_col0 = B_COL0 + 3 * B_W + B_VW
    gblk = lambda k: pl.BlockSpec((tm, B_VW // 2), lambda i: (i, g_col0 // (B_VW // 2) + k))
    return pl.pallas_call(
        _out_proj_body,
        grid=(L // tm,),
        in_specs=[row(D_MODEL), row(ya.shape[1]), row(B_VW), row(B_VW), gblk(0), gblk(1),
                  row(yc.shape[1]), full(wa), full(wb), full(wc), full(gw),
                  pl.BlockSpec((1, D_MODEL), lambda i: (0, 0))],
        out_specs=row(D_MODEL),
        out_shape=jax.ShapeDtypeStruct((L, D_MODEL), F32),
        compiler_params=_params("parallel"),
        name="out_proj",
    )(x, ya, o_fwd, o_bwd, proj, proj, yc, wa, wb, wc, gw, nw)


def _split_w_out(w_out):
    wa = w_out[:A_Q]
    wb = w_out[A_Q:A_Q + B_VW]
    wc = w_out[A_Q + B_VW:].reshape(C_KV_HEADS, GROUP * HEAD_DIM, D_MODEL)
    wc = jnp.pad(wc, ((0, 0), (0, 2 * LANES - GROUP * HEAD_DIM), (0, 0)))
    return wa.astype(BF16), wb.astype(BF16), wc.reshape(C_KV_HEADS * 2 * LANES, D_MODEL).astype(BF16)


def _t5_bucket(rel):
    nb = REL_BUCKETS // 2
    max_exact = nb // 2
    n = jnp.abs(rel)
    nf = jnp.maximum(n, 1).astype(F32)
    large = max_exact + (jnp.log(nf / max_exact) / math.log(REL_MAX_DIST / max_exact)
                         * (nb - max_exact)).astype(jnp.int32)
    large = jnp.minimum(large, nb - 1)
    return jnp.where(rel > 0, nb, 0) + jnp.where(n < max_exact, n, large)


def _rope_tables(L):
    half = HEAD_DIM // 2
    inv = 1.0 / (ROPE_THETA ** (jnp.arange(0, half, 2, dtype=F32) / half))
    pos = jnp.arange(L)
    ang = jnp.concatenate([(pos // GRID_W).astype(F32)[:, None] * inv,
                           (pos % GRID_W).astype(F32)[:, None] * inv], axis=-1)
    cos = jnp.repeat(jnp.cos(ang), 2, axis=-1)
    sin = jnp.repeat(jnp.sin(ang), 2, axis=-1) * jnp.tile(jnp.array([-1.0, 1.0], F32), half)
    return jnp.tile(cos, (1, LANES // HEAD_DIM)), jnp.tile(sin, (1, LANES // HEAD_DIM))


def _band_bias(rel_bias):
    period = 4 * A_BLOCK
    rel = jnp.arange(period) - A_BLOCK
    vals = rel_bias.astype(F32)[_t5_bucket(rel)] * LOG2E
    vals = jnp.where((jnp.abs(rel) <= WINDOW)[:, None], vals, NEG).T
    flat = jnp.tile(vals, (1, A_BLOCK))[:, :A_BLOCK * (period - 1)]
    return flat.reshape(A_HEADS, A_BLOCK, period - 1)[:, :, :3 * A_BLOCK]


def kernel(x, w_in, w_out, ffn1_gate, ffn1_up, ffn1_down, ffn2_gate, ffn2_up, ffn2_down,
           norm_w, sink_logits, qk_norm_w, hgrn_lb, hgrn_norm_w, rel_bias):
    B_, L, _ = x.shape
    depth = w_in.shape[0]
    bias = _band_bias(rel_bias)
    cos_t, sin_t = _rope_tables(L)
    lb_c = jnp.cumsum(jax.nn.softmax(hgrn_lb.astype(F32), axis=0), axis=0)
    lbs = lb_c - lb_c[0:1]
    qkw = jnp.tile(qk_norm_w.astype(F32), (1, 1, LANES // HEAD_DIM))
    nw = norm_w.astype(F32).reshape(depth, 6, 1, D_MODEL)
    ffn1_b = (ffn1_gate, ffn1_up, ffn1_down)
    ffn2_b = (ffn2_gate, ffn2_up, ffn2_down)

    outs = []
    for b in range(B_):
        xb = x[b]
        for l in range(depth):
            xb = _ffn(xb, nw[l, 0], nw[l, 1], *ffn1_b, l)
            proj = _in_proj(xb, nw[l, 2], w_in, l)
            ya = _win_attn(proj, bias,
                           jnp.broadcast_to((sink_logits[l].astype(F32) * LOG2E)
                                            .reshape(A_HEADS, 1, 1), (A_HEADS, 1, LANES)))
            o_fwd, o_bwd = _hgrn(proj, lbs[l])
            q, kt, v = _rope_prep(proj, qkw[l, 0:1], qkw[l, 1:2], cos_t, sin_t)
            yc = _axial_attn(q, kt, v)
            oa, ob, oc = _split_w_out(w_out[l])
            xb = _out_proj(xb, ya, o_fwd, o_bwd, proj, yc, oa, ob, oc,
                           hgrn_norm_w[l].astype(F32).reshape(1, B_DV), nw[l, 3])
            xb = _ffn(xb, nw[l, 4], nw[l, 5], *ffn2_b, l)
        outs.append(xb)
    return jnp.stack(outs, axis=0)
```

```python
import functools
import math

import jax
import jax.numpy as jnp
from jax import lax
from jax.experimental import pallas as pl
from jax.experimental.pallas import tpu as pltpu

D_MODEL = 2048
HEAD_DIM = 64
A_HEADS = 12
A_KV_HEADS = 4
WINDOW = 128
A_BLOCK = 128
B_HEADS = 4
B_DK = 128
B_DV = 128
C_HEADS = 12
C_KV_HEADS = 4
ROPE_THETA = 10000.0
GRID_W = 64
REL_BUCKETS = 32
REL_MAX_DIST = 128
D_FF = 5632
EPS = 1e-6
NEG = -1e30

A_Q = A_HEADS * HEAD_DIM
A_KV = A_KV_HEADS * HEAD_DIM
B_W = B_HEADS * B_DK
B_VW = B_HEADS * B_DV
C_Q = C_HEADS * HEAD_DIM
C_KV = C_KV_HEADS * HEAD_DIM
B_COL0 = A_Q + 2 * A_KV

LANES = 128
SUBLANES = 8
ROW_CHUNK = 2 * SUBLANES
Q_PAD = A_HEADS * LANES
KV_PAD = A_KV_HEADS * LANES
GROUP = A_HEADS // A_KV_HEADS
VMEM_LIMIT = 52 * 1024 * 1024
FFN_ROWS = 1024
FFN_COLS = 256
FFN_VMEM_LIMIT = 58 * 1024 * 1024

LOG2E = math.log2(math.e)
AXIAL_UNROLL = 8

WIN_QBLOCKS = 2
HGRN_CHUNK = 128
HGRN_IN_WIDTH = B_COL0

BF16 = jnp.bfloat16
F32 = jnp.float32


def _params(*sem):
    return pltpu.CompilerParams(dimension_semantics=sem, vmem_limit_bytes=VMEM_LIMIT)


def _dot(a, b):
    return jnp.dot(a, b, preferred_element_type=F32)


def _dot_nt(a, b):
    return lax.dot_general(a, b, (((1,), (1,)), ((), ())), preferred_element_type=F32)


def _rms(x, w):
    sq = x * x
    width = x.shape[-1]
    folded = functools.reduce(
        jnp.add, [sq[:, i * LANES:(i + 1) * LANES] for i in range(width // LANES)])
    ms = jnp.sum(folded, axis=-1, keepdims=True) * (1.0 / width)
    return x * lax.rsqrt(ms + EPS) * w


def _for_row_chunks(nrows, fn):
    for start in range(0, nrows, ROW_CHUNK):
        fn(slice(start, start + ROW_CHUNK))


def _sigmoid(x):
    return 1.0 / (1.0 + jnp.exp(-x))


def _ffn_body(nf, x_ref, prew_ref, postw_ref, wg_ref, wu_ref, wd_ref, o_ref, h_scr):
    j = pl.program_id(1)

    nrows = x_ref.shape[0]

    @pl.when(j == 0)
    def _():
        def pre(rows):
            h_scr[rows, :] = _rms(x_ref[rows, :], prew_ref[...]).astype(BF16)
            o_ref[rows, :] = jnp.zeros((ROW_CHUNK, D_MODEL), F32)

        _for_row_chunks(nrows, pre)

    h = h_scr[...]
    g = _dot(h, wg_ref[...].astype(BF16))
    u = _dot(h, wu_ref[...].astype(BF16))
    a = (g * _sigmoid(g)) * u
    o_ref[...] += _dot(a.astype(BF16), wd_ref[...].astype(BF16))

    @pl.when(j == nf - 1)
    def _():
        def post(rows):
            o_ref[rows, :] = x_ref[rows, :] + 0.5 * _rms(o_ref[rows, :], postw_ref[...])

        _for_row_chunks(nrows, post)


def _ffn(x, prew, postw, wg, wu, wd, layer):
    L = x.shape[0]
    tm = min(FFN_ROWS, L)
    tf = FFN_COLS
    nf = D_FF // tf
    row = pl.BlockSpec((tm, D_MODEL), lambda i, j: (i, 0))
    vec = pl.BlockSpec((1, D_MODEL), lambda i, j: (0, 0))
    return pl.pallas_call(
        functools.partial(_ffn_body, nf),
        grid=(L // tm, nf),
        in_specs=[row, vec, vec,
                  pl.BlockSpec((None, D_MODEL, tf), lambda i, j: (layer, 0, j)),
                  pl.BlockSpec((None, D_MODEL, tf), lambda i, j: (layer, 0, j)),
                  pl.BlockSpec((None, tf, D_MODEL), lambda i, j: (layer, j, 0))],
        out_specs=row,
        out_shape=jax.ShapeDtypeStruct((L, D_MODEL), F32),
        scratch_shapes=[pltpu.VMEM((tm, D_MODEL), BF16)],
        compiler_params=pltpu.CompilerParams(dimension_semantics=("parallel", "arbitrary"),
                                             vmem_limit_bytes=FFN_VMEM_LIMIT),
        name="ffn",
    )(x, prew, postw, wg, wu, wd)


def _in_proj_body(x_ref, nw_ref, w_ref, o_ref, h_scr):
    @pl.when(pl.program_id(1) == 0)
    def _():
        def pre(rows):
            h_scr[rows, :] = _rms(x_ref[rows, :], nw_ref[...]).astype(BF16)

        _for_row_chunks(x_ref.shape[0], pre)

    o_ref[...] = _dot(h_scr[...], w_ref[...].astype(BF16))


def _in_proj(x, nw, w, layer):
    L = x.shape[0]
    N = w.shape[2]
    tm = min(2048, L)
    tn = 512
    return pl.pallas_call(
        _in_proj_body,
        grid=(L // tm, N // tn),
        in_specs=[pl.BlockSpec((tm, D_MODEL), lambda i, j: (i, 0), pipeline_mode=pl.Buffered(1)),
                  pl.BlockSpec((1, D_MODEL), lambda i, j: (0, 0)),
                  pl.BlockSpec((None, D_MODEL, tn), lambda i, j: (layer, 0, j))],
        out_specs=pl.BlockSpec((tm, tn), lambda i, j: (i, j)),
        out_shape=jax.ShapeDtypeStruct((L, N), F32),
        scratch_shapes=[pltpu.VMEM((tm, D_MODEL), BF16)],
        compiler_params=_params("parallel", "arbitrary"),
        name="in_proj",
    )(x, nw, w)


def _win_attn_body(nb, q_ref, *refs):
    nkb = WIN_QBLOCKS + 2
    k_refs, v_refs = refs[:nkb], refs[nkb:2 * nkb]
    bias_ref, sink_ref, o_ref = refs[2 * nkb:]
    k_blocks = [r[...].astype(BF16) for r in k_refs]
    v_blocks = [r[...].astype(BF16) for r in v_refs]
    for b in range(WIN_QBLOCKS):
        rows = slice(b * A_BLOCK, (b + 1) * A_BLOCK)
        _win_attn_block(pl.program_id(0) * WIN_QBLOCKS + b, nb, q_ref.at[rows, :],
                        jnp.concatenate(k_blocks[b:b + 3], axis=0),
                        jnp.concatenate(v_blocks[b:b + 3], axis=0),
                        bias_ref, sink_ref, o_ref.at[rows, :])


def _win_attn_block(n, nb, q_ref, k_all, v_all, bias_ref, sink_ref, o_ref):
    edge_lo = jnp.where(n == 0, NEG, 0.0)
    edge_hi = jnp.where(n == nb - 1, NEG, 0.0)
    low = lax.broadcasted_iota(jnp.int32, (A_BLOCK, LANES), 1) < HEAD_DIM
    tile = lambda i: slice(i * LANES, (i + 1) * LANES)
    outs = []
    for g in range(A_KV_HEADS):
        heads = range(g * GROUP, (g + 1) * GROUP)
        in_kv_half = low if g % 2 == 0 else jnp.logical_not(low)
        qs = []
        for h in heads:
            qh = q_ref[:, tile(h // 2)] * (LOG2E * HEAD_DIM ** -0.5)
            if h % 2 != g % 2:
                qh = pltpu.roll(qh, HEAD_DIM, axis=1)
            qs.append(jnp.where(in_kv_half, qh, 0.0).astype(BF16))
        s3 = _dot_nt(jnp.concatenate(qs, axis=0), k_all[:, tile(g // 2)])
        ps, dens = [], []
        for j, h in enumerate(heads):
            s = s3[j * A_BLOCK:(j + 1) * A_BLOCK] + bias_ref[h]
            t = [s[:, 0:LANES] + edge_lo, s[:, LANES:2 * LANES], s[:, 2 * LANES:] + edge_hi]
            sink = sink_ref[h]
            m = jnp.maximum(jnp.max(jnp.maximum(jnp.maximum(t[0], t[1]), t[2]), axis=-1,
                                    keepdims=True), sink)
            p = [jnp.exp2(ti - m) for ti in t]
            dens.append(jnp.sum(p[0] + p[1] + p[2], axis=-1, keepdims=True) + jnp.exp2(sink - m))
            ps.append(jnp.concatenate(p, axis=1).astype(BF16))
        o3 = _dot(jnp.concatenate(ps, axis=0), v_all[:, tile(g // 2)])
        for j, h in enumerate(heads):
            o = o3[j * A_BLOCK:(j + 1) * A_BLOCK] / dens[j]
            outs.append(o if h % 2 == g % 2 else pltpu.roll(o, HEAD_DIM, axis=1))
    for pr in range(A_HEADS // 2):
        o_ref[:, tile(pr)] = jnp.where(low, outs[2 * pr], outs[2 * pr + 1]).astype(o_ref.dtype)


def _win_attn(proj, bias, sink):
    L = proj.shape[0]
    nb = L // A_BLOCK
    qb = WIN_QBLOCKS
    kcol, vcol = A_Q // A_KV, A_Q // A_KV + 1
    kblk = lambda r, c: pl.BlockSpec(
        (A_BLOCK, A_KV), lambda n: (jnp.clip(n * qb - 1 + r, 0, nb - 1), c))
    kv_specs = [kblk(r, c) for c in (kcol, vcol) for r in range(qb + 2)]
    return pl.pallas_call(
        functools.partial(_win_attn_body, nb),
        grid=(nb // qb,),
        in_specs=[pl.BlockSpec((qb * A_BLOCK, A_Q), lambda n: (n, 0))] + kv_specs + [
            pl.BlockSpec((A_HEADS, A_BLOCK, 3 * A_BLOCK), lambda n: (0, 0, 0)),
            pl.BlockSpec((A_HEADS, 1, LANES), lambda n: (0, 0, 0))],
        out_specs=pl.BlockSpec((qb * A_BLOCK, A_Q), lambda n: (n, 0)),
        out_shape=jax.ShapeDtypeStruct((L, A_Q), BF16),
        compiler_params=_params("parallel"),
        name="win_attn",
    )(proj, *([proj] * len(kv_specs)), bias, sink)


def _cumsum_rows(tri, g):
    g1 = g.astype(BF16)
    r1 = g - g1.astype(F32)
    g2 = r1.astype(BF16)
    g3 = (r1 - g2.astype(F32)).astype(BF16)
    return _dot(tri, g1) + _dot(tri, g2) + _dot(tri, g3)


def _hgrn_tables(reverse):
    C = HGRN_CHUNK
    row = lax.broadcasted_iota(jnp.int32, (C, C), 0)
    col = lax.broadcasted_iota(jnp.int32, (C, C), 1)
    tri = ((col >= row) if reverse else (col <= row)).astype(BF16)
    diff = (row ^ col).astype(F32)
    high = (lax.bitcast_convert_type(diff, jnp.int32) >> 23) - 127
    wrong_side = (row < col) if not reverse else (row > col)
    level = jnp.where(row == col, 0, jnp.where(wrong_side, -1, high + 1))
    return tri, level


def _hgrn_chunk(reverse, q, z, v, lb, st_prev, tri, level, later_small):
    C = HGRN_CHUNK
    log_lb = jnp.log(lb)
    log_sig = jnp.minimum(z, 0.0) - jnp.log(1.0 + jnp.exp(-jnp.abs(z)))
    c2 = jnp.log1p(-lb) + log_sig
    g = jnp.maximum(log_lb, c2) + jnp.log(1.0 + jnp.exp(-jnp.abs(log_lb - c2)))
    k = (1.0 - lb) * (1.0 / (1.0 + jnp.exp(z)))

    b = _cumsum_rows(tri, g) * LOG2E
    btot = b[0:1, :] if reverse else b[C - 1:C, :]

    terms = [_dot_nt(q.astype(BF16), k.astype(BF16))]
    seg = b
    for j in range(int(math.log2(C))):
        m = 1 << j
        if m < SUBLANES:
            later = later_small[j] != 0
            up = pltpu.roll(seg, m, axis=0)
            down = pltpu.roll(seg, C - m, axis=0)
            if reverse:
                ref = jnp.where(later, seg, down)
                seg = jnp.where(later, up, seg)
            else:
                ref = jnp.where(later, up, seg)
                seg = jnp.where(later, seg, down)
            x = jnp.exp2(-jnp.abs(b - ref))
        else:
            expo, nxt = [], []
            for p in range(C // (2 * m)):
                lo, hi = slice(2 * p * m, (2 * p + 1) * m), slice((2 * p + 1) * m, (2 * p + 2) * m)
                if reverse:
                    expo += [b[lo] - seg[hi], seg[hi] - b[hi]]
                    nxt += [seg[lo], seg[lo]]
                else:
                    expo += [seg[lo] - b[lo], b[hi] - seg[lo]]
                    nxt += [seg[hi], seg[hi]]
            x = jnp.exp2(jnp.concatenate(expo, axis=0))
            seg = jnp.concatenate(nxt, axis=0)
        terms.append(_dot_nt((q * x).astype(BF16), (k * x).astype(BF16)))
    att = jnp.zeros((C, C), F32)
    for i in reversed(range(len(terms))):
        att = jnp.where(level == i, terms[i], att)
    vb = v.astype(BF16)
    o = _dot(att.astype(BF16), vb)

    o = o + _dot_nt((q * jnp.exp2(b)).astype(BF16), st_prev.astype(BF16))
    kd = (k * jnp.exp2(btot - b)).astype(BF16)
    st_next = jnp.exp2(btot) * st_prev + _dot(v.T.astype(BF16), kd)
    return o, st_next


def _hgrn_body(nch, fwd0_ref, fwd1_ref, bwd0_ref, bwd1_ref, lb_ref, of_ref, ob_ref,
               sf_scr, sb_scr):
    @pl.when(pl.program_id(0) == 0)
    def _():
        sf_scr[...] = jnp.zeros_like(sf_scr)
        sb_scr[...] = jnp.zeros_like(sb_scr)

    C = HGRN_CHUNK
    tables = [_hgrn_tables(False), _hgrn_tables(True)]
    rowv = lax.broadcasted_iota(jnp.int32, (C, LANES), 0)
    later_small = [(rowv >> j) & 1 for j in range(int(math.log2(SUBLANES)))]

    def one(reverse, c, h, halves, z_part, o_ref, s_scr):
        ci = (nch - 1 - c) if reverse else c
        rows = pl.ds(pl.multiple_of(ci * C, C), C)

        def load(part):
            half, off = divmod(part * B_W + h * B_DK, HGRN_IN_WIDTH)
            return halves[half][rows, off:off + B_DK]

        qr = load(0)
        tri, level = tables[int(reverse)]
        o, s_next = _hgrn_chunk(reverse, qr * _sigmoid(qr), load(z_part), load(3),
                                lb_ref[int(reverse), h], s_scr[h], tri, level, later_small)
        s_scr[h] = s_next
        o_ref[rows, h * B_DV:(h + 1) * B_DV] = o

    for c in range(nch):
        for h in range(B_HEADS):
            one(False, c, h, (fwd0_ref, fwd1_ref), 1, of_ref, sf_scr)
            one(True, c, h, (bwd0_ref, bwd1_ref), 2, ob_ref, sb_scr)


def _hgrn(proj, lb2):
    L = proj.shape[0]
    tl = min(512, L)
    nt = L // tl
    nch = tl // HGRN_CHUNK
    lb4 = lb2.reshape(2, B_HEADS, 1, B_DK)
    first = B_COL0 // HGRN_IN_WIDTH
    fwd = lambda k: pl.BlockSpec((tl, HGRN_IN_WIDTH), lambda t: (t, first + k))
    bwd = lambda k: pl.BlockSpec((tl, HGRN_IN_WIDTH), lambda t: (nt - 1 - t, first + k))
    state = pltpu.VMEM((B_HEADS, B_DV, B_DK), F32)
    return pl.pallas_call(
        functools.partial(_hgrn_body, nch),
        grid=(nt,),
        in_specs=[fwd(0), fwd(1), bwd(0), bwd(1),
                  pl.BlockSpec((2, B_HEADS, 1, B_DK), lambda t: (0, 0, 0, 0))],
        out_specs=[pl.BlockSpec((tl, B_VW), lambda t: (t, 0)),
                   pl.BlockSpec((tl, B_VW), lambda t: (nt - 1 - t, 0))],
        out_shape=[jax.ShapeDtypeStruct((L, B_VW), F32), jax.ShapeDtypeStruct((L, B_VW), F32)],
        scratch_shapes=[state, state],
        compiler_params=_params("arbitrary"),
        name="hgrn",
    )(proj, proj, proj, proj, lb4)


def _rope_prep_body(x_ref, qw_ref, kw_ref, cos_ref, sin_ref, q_ref, kt_ref, v_ref):
    cos = cos_ref[...]
    sin = sin_ref[...]
    lane = lax.broadcasted_iota(jnp.int32, cos.shape, 1)
    even = (lane % 2) == 0
    low = lane < HEAD_DIM
    ones_lane = jnp.where(lane == HEAD_DIM, 1.0, 0.0)
    tile = lambda i: slice(i * LANES, (i + 1) * LANES)
    halves = lambda y: (y, pltpu.roll(y, HEAD_DIM, axis=1))

    def norm_rope(x, w):
        sq = x * x
        ms = jnp.where(low, jnp.sum(jnp.where(low, sq, 0.0), axis=-1, keepdims=True),
                       jnp.sum(jnp.where(low, 0.0, sq), axis=-1, keepdims=True)) * (1.0 / HEAD_DIM)
        y = x * lax.rsqrt(ms + EPS) * w
        partner = jnp.where(even, pltpu.roll(y, LANES - 1, axis=1), pltpu.roll(y, 1, axis=1))
        return y * cos + partner * sin

    for t in range(C_Q // LANES):
        y = norm_rope(x_ref[:, tile(t)], qw_ref[...]) * (LOG2E * HEAD_DIM ** -0.5)
        for half, yh in enumerate(halves(y)):
            q_ref[:, tile(2 * t + half)] = yh.astype(q_ref.dtype)
    for t in range(C_KV // LANES):
        y = norm_rope(x_ref[:, tile(C_Q // LANES + t)], kw_ref[...])
        v = x_ref[:, tile((C_Q + C_KV) // LANES + t)]
        for half, (yh, vh) in enumerate(zip(halves(y), halves(v))):
            kt_ref[tile(2 * t + half), :] = jnp.where(low, yh, 0.0).T.astype(kt_ref.dtype)
            v_ref[:, tile(2 * t + half)] = jnp.where(low, vh, ones_lane).astype(v_ref.dtype)


def _rope_prep(proj, qw, kw, cos_t, sin_t):
    L = proj.shape[0]
    tl = min(256, L)
    width = C_Q + 2 * C_KV
    vec = pl.BlockSpec((1, LANES), lambda i: (0, 0))
    tab = pl.BlockSpec((tl, LANES), lambda i: (i, 0))
    return pl.pallas_call(
        _rope_prep_body,
        grid=(L // tl,),
        in_specs=[pl.BlockSpec((tl, width), lambda i: (i, proj.shape[1] // width - 1)),
                  vec, vec, tab, tab],
        out_specs=[pl.BlockSpec((tl, Q_PAD), lambda i: (i, 0)),
                   pl.BlockSpec((KV_PAD, tl), lambda i: (0, i)),
                   pl.BlockSpec((tl, KV_PAD), lambda i: (i, 0))],
        out_shape=[jax.ShapeDtypeStruct((L, Q_PAD), BF16),
                   jax.ShapeDtypeStruct((KV_PAD, L), BF16),
                   jax.ShapeDtypeStruct((L, KV_PAD), BF16)],
        compiler_params=_params("parallel"),
        name="rope_prep",
    )(proj, qw, kw, cos_t, sin_t)


def _axial_attn_body(tq, tk, nk, q_ref, kt_ref, v_ref, o_ref, m_scr, acc_scr):
    q = jnp.concatenate([q_ref[:, h * LANES:(h + 1) * LANES] for h in range(GROUP)], axis=0)
    m_scr[...] = jnp.full(m_scr.shape, -jnp.inf, F32)
    acc_scr[...] = jnp.zeros(acc_scr.shape, F32)

    def step(t, carry):
        k0 = pl.multiple_of(t * tk, tk)
        s = _dot(q, kt_ref[:, pl.ds(k0, tk)])
        parts = [s[:, i * LANES:(i + 1) * LANES] for i in range(tk // LANES)]
        m = m_scr[...]
        m_new = jnp.maximum(m, jnp.max(functools.reduce(jnp.maximum, parts), axis=-1,
                                       keepdims=True))
        p = jnp.concatenate([jnp.exp2(part - m_new) for part in parts], axis=1)
        acc_scr[...] = jnp.exp2(m - m_new) * acc_scr[...] + _dot(p.astype(BF16),
                                                                  v_ref[pl.ds(k0, tk), :])
        m_scr[...] = m_new
        return carry

    lax.fori_loop(0, nk, step, 0, unroll=AXIAL_UNROLL)
    acc = acc_scr[...]
    lane = lax.broadcasted_iota(jnp.int32, acc.shape, 1)
    o = jnp.where(lane < HEAD_DIM, acc / acc[:, HEAD_DIM:HEAD_DIM + 1], 0.0)
    o0, o1, o2 = o[0:tq], o[tq:2 * tq], o[2 * tq:3 * tq]
    o_ref[:, 0:LANES] = (o0 + pltpu.roll(o1, HEAD_DIM, axis=1)).astype(o_ref.dtype)
    o_ref[:, LANES:2 * LANES] = o2.astype(o_ref.dtype)


def _axial_attn(q, kt, v):
    L = q.shape[0]
    tq = min(1024, L)
    tk = min(512, L)
    return pl.pallas_call(
        functools.partial(_axial_attn_body, tq, tk, L // tk),
        grid=(C_KV_HEADS, L // tq),
        in_specs=[pl.BlockSpec((tq, GROUP * LANES), lambda g, i: (i, g)),
                  pl.BlockSpec((LANES, L), lambda g, i: (g, 0)),
                  pl.BlockSpec((L, LANES), lambda g, i: (0, g))],
        out_specs=pl.BlockSpec((tq, 2 * LANES), lambda g, i: (i, g)),
        out_shape=jax.ShapeDtypeStruct((L, C_KV_HEADS * 2 * LANES), BF16),
        scratch_shapes=[pltpu.VMEM((GROUP * tq, LANES), F32), pltpu.VMEM((GROUP * tq, LANES), F32)],
        compiler_params=_params("parallel", "arbitrary"),
        name="axial_attn",
    )(q, kt, v)


def _out_proj_body(x_ref, ya_ref, of_ref, ob_ref, g0_ref, g1_ref, yc_ref, wa_ref, wb_ref, wc_ref,
                   gw_ref, nw_ref, o_ref):
    tot = of_ref[...] + ob_ref[...]
    normed = jnp.concatenate([_rms(tot[:, h * B_DV:(h + 1) * B_DV], gw_ref[...])
                              for h in range(B_HEADS)], axis=1)
    g = jnp.concatenate([g0_ref[...], g1_ref[...]], axis=1)
    yb = (normed * (g * _sigmoid(g))).astype(BF16)
    y = _dot(ya_ref[...], wa_ref[...]) + _dot(yb, wb_ref[...]) + _dot(yc_ref[...], wc_ref[...])
    o_ref[...] = x_ref[...] + _rms(y, nw_ref[...])


def _out_proj(x, ya, o_fwd, o_bwd, proj, yc, wa, wb, wc, gw, nw):
    L = x.shape[0]
    tm = min(512, L)
    row = lambda w: pl.BlockSpec((tm, w), lambda i: (i, 0))
    full = lambda a: pl.BlockSpec(a.shape, lambda i: (0, 0))
    g_col0 = B_COL0 + 3 * B_W + B_VW
    gblk = lambda k: pl.BlockSpec((tm, B_VW // 2), lambda i: (i, g_col0 // (B_VW // 2) + k))
    return pl.pallas_call(
        _out_proj_body,
        grid=(L // tm,),
        in_specs=[row(D_MODEL), row(ya.shape[1]), row(B_VW), row(B_VW), gblk(0), gblk(1),
                  row(yc.shape[1]), full(wa), full(wb), full(wc), full(gw),
                  pl.BlockSpec((1, D_MODEL), lambda i: (0, 0))],
        out_specs=row(D_MODEL),
        out_shape=jax.ShapeDtypeStruct((L, D_MODEL), F32),
        compiler_params=_params("parallel"),
        name="out_proj",
    )(x, ya, o_fwd, o_bwd, proj, proj, yc, wa, wb, wc, gw, nw)


def _split_w_out(w_out):
    wa = w_out[:A_Q]
    wb = w_out[A_Q:A_Q + B_VW]
    wc = w_out[A_Q + B_VW:].reshape(C_KV_HEADS, GROUP * HEAD_DIM, D_MODEL)
    wc = jnp.pad(wc, ((0, 0), (0, 2 * LANES - GROUP * HEAD_DIM), (0, 0)))
    return wa.astype(BF16), wb.astype(BF16), wc.reshape(C_KV_HEADS * 2 * LANES, D_MODEL).astype(BF16)


def _t5_bucket(rel):
    nb = REL_BUCKETS // 2
    max_exact = nb // 2
    n = jnp.abs(rel)
    nf = jnp.maximum(n, 1).astype(F32)
    large = max_exact + (jnp.log(nf / max_exact) / math.log(REL_MAX_DIST / max_exact)
                         * (nb - max_exact)).astype(jnp.int32)
    large = jnp.minimum(large, nb - 1)
    return jnp.where(rel > 0, nb, 0) + jnp.where(n < max_exact, n, large)


def _rope_tables(L):
    half = HEAD_DIM // 2
    inv = 1.0 / (ROPE_THETA ** (jnp.arange(0, half, 2, dtype=F32) / half))
    pos = jnp.arange(L)
    ang = jnp.concatenate([(pos // GRID_W).astype(F32)[:, None] * inv,
                           (pos % GRID_W).astype(F32)[:, None] * inv], axis=-1)
    cos = jnp.repeat(jnp.cos(ang), 2, axis=-1)
    sin = jnp.repeat(jnp.sin(ang), 2, axis=-1) * jnp.tile(jnp.array([-1.0, 1.0], F32), half)
    return jnp.tile(cos, (1, LANES // HEAD_DIM)), jnp.tile(sin, (1, LANES // HEAD_DIM))


def _band_bias(rel_bias):
    period = 4 * A_BLOCK
    rel = jnp.arange(period) - A_BLOCK
    vals = rel_bias.astype(F32)[_t5_bucket(rel)] * LOG2E
    vals = jnp.where((jnp.abs(rel) <= WINDOW)[:, None], vals, NEG).T
    flat = jnp.tile(vals, (1, A_BLOCK))[:, :A_BLOCK * (period - 1)]
    return flat.reshape(A_HEADS, A_BLOCK, period - 1)[:, :, :3 * A_BLOCK]


def kernel(x, w_in, w_out, ffn1_gate, ffn1_up, ffn1_down, ffn2_gate, ffn2_up, ffn2_down,
           norm_w, sink_logits, qk_norm_w, hgrn_lb, hgrn_norm_w, rel_bias):
    B_, L, _ = x.shape
    depth = w_in.shape[0]
    bias = _band_bias(rel_bias)
    cos_t, sin_t = _rope_tables(L)
    lb_c = jnp.cumsum(jax.nn.softmax(hgrn_lb.astype(F32), axis=0), axis=0)
    lbs = lb_c - lb_c[0:1]
    qkw = jnp.tile(qk_norm_w.astype(F32), (1, 1, LANES // HEAD_DIM))
    nw = norm_w.astype(F32).reshape(depth, 6, 1, D_MODEL)
    ffn1_b = (ffn1_gate, ffn1_up, ffn1_down)
    ffn2_b = (ffn2_gate, ffn2_up, ffn2_down)

    outs = []
    for b in range(B_):
        xb = x[b]
        for l in range(depth):
            xb = _ffn(xb, nw[l, 0], nw[l, 1], *ffn1_b, l)
            proj = _in_proj(xb, nw[l, 2], w_in, l)
            ya = _win_attn(proj, bias,
                           jnp.broadcast_to((sink_logits[l].astype(F32) * LOG2E)
                                            .reshape(A_HEADS, 1, 1), (A_HEADS, 1, LANES)))
            o_fwd, o_bwd = _hgrn(proj, lbs[l])
            q, kt, v = _rope_prep(proj, qkw[l, 0:1], qkw[l, 1:2], cos_t, sin_t)
            yc = _axial_attn(q, kt, v)
            oa, ob, oc = _split_w_out(w_out[l])
            xb = _out_proj(xb, ya, o_fwd, o_bwd, proj, yc, oa, ob, oc,
                           hgrn_norm_w[l].astype(F32).reshape(1, B_DV), nw[l, 3])
            xb = _ffn(xb, nw[l, 4], nw[l, 5], *ffn2_b, l)
        outs.append(xb)
    return jnp.stack(outs, axis=0)
```

```python
import functools
import math

import jax
import jax.numpy as jnp
from jax import lax
from jax.experimental import pallas as pl
from jax.experimental.pallas import tpu as pltpu

D_MODEL = 2048
HEAD_DIM = 64
A_HEADS = 12
A_KV_HEADS = 4
WINDOW = 128
A_BLOCK = 128
B_HEADS = 4
B_DK = 128
B_DV = 128
C_HEADS = 12
C_KV_HEADS = 4
ROPE_THETA = 10000.0
GRID_W = 64
REL_BUCKETS = 32
REL_MAX_DIST = 128
D_FF = 5632
EPS = 1e-6
NEG = -1e30

A_Q = A_HEADS * HEAD_DIM
A_KV = A_KV_HEADS * HEAD_DIM
B_W = B_HEADS * B_DK
B_VW = B_HEADS * B_DV
C_Q = C_HEADS * HEAD_DIM
C_KV = C_KV_HEADS * HEAD_DIM
B_COL0 = A_Q + 2 * A_KV

LANES = 128
SUBLANES = 8
ROW_CHUNK = 2 * SUBLANES
Q_PAD = A_HEADS * LANES
KV_PAD = A_KV_HEADS * LANES
GROUP = A_HEADS // A_KV_HEADS
VMEM_LIMIT = 52 * 1024 * 1024
FFN_ROWS = 1024
FFN_COLS = 256
FFN_VMEM_LIMIT = 58 * 1024 * 1024

LOG2E = math.log2(math.e)
AXIAL_UNROLL = 16

WIN_QBLOCKS = 2
HGRN_CHUNK = 128
HGRN_IN_WIDTH = B_COL0

BF16 = jnp.bfloat16
F32 = jnp.float32


def _params(*sem):
    return pltpu.CompilerParams(dimension_semantics=sem, vmem_limit_bytes=VMEM_LIMIT)


def _dot(a, b):
    return jnp.dot(a, b, preferred_element_type=F32)


def _dot_nt(a, b):
    return lax.dot_general(a, b, (((1,), (1,)), ((), ())), preferred_element_type=F32)


def _rms(x, w):
    sq = x * x
    width = x.shape[-1]
    folded = functools.reduce(
        jnp.add, [sq[:, i * LANES:(i + 1) * LANES] for i in range(width // LANES)])
    ms = jnp.sum(folded, axis=-1, keepdims=True) * (1.0 / width)
    return x * lax.rsqrt(ms + EPS) * w


def _for_row_chunks(nrows, fn):
    for start in range(0, nrows, ROW_CHUNK):
        fn(slice(start, start + ROW_CHUNK))


def _sigmoid(x):
    return 1.0 / (1.0 + jnp.exp(-x))


def _ffn_body(nf, x_ref, prew_ref, postw_ref, wg_ref, wu_ref, wd_ref, o_ref, h_scr):
    j = pl.program_id(1)

    nrows = x_ref.shape[0]

    @pl.when(j == 0)
    def _():
        def pre(rows):
            h_scr[rows, :] = _rms(x_ref[rows, :], prew_ref[...]).astype(BF16)
            o_ref[rows, :] = jnp.zeros((ROW_CHUNK, D_MODEL), F32)

        _for_row_chunks(nrows, pre)

    h = h_scr[...]
    g = _dot(h, wg_ref[...].astype(BF16))
    u = _dot(h, wu_ref[...].astype(BF16))
    a = (g * _sigmoid(g)) * u
    o_ref[...] += _dot(a.astype(BF16), wd_ref[...].astype(BF16))

    @pl.when(j == nf - 1)
    def _():
        def post(rows):
            o_ref[rows, :] = x_ref[rows, :] + 0.5 * _rms(o_ref[rows, :], postw_ref[...])

        _for_row_chunks(nrows, post)


def _ffn(x, prew, postw, wg, wu, wd, layer):
    L = x.shape[0]
    tm = min(FFN_ROWS, L)
    tf = FFN_COLS
    nf = D_FF // tf
    row = pl.BlockSpec((tm, D_MODEL), lambda i, j: (i, 0))
    vec = pl.BlockSpec((1, D_MODEL), lambda i, j: (0, 0))
    return pl.pallas_call(
        functools.partial(_ffn_body, nf),
        grid=(L // tm, nf),
        in_specs=[row, vec, vec,
                  pl.BlockSpec((None, D_MODEL, tf), lambda i, j: (layer, 0, j)),
                  pl.BlockSpec((None, D_MODEL, tf), lambda i, j: (layer, 0, j)),
                  pl.BlockSpec((None, tf, D_MODEL), lambda i, j: (layer, j, 0))],
        out_specs=row,
        out_shape=jax.ShapeDtypeStruct((L, D_MODEL), F32),
        scratch_shapes=[pltpu.VMEM((tm, D_MODEL), BF16)],
        compiler_params=pltpu.CompilerParams(dimension_semantics=("parallel", "arbitrary"),
                                             vmem_limit_bytes=FFN_VMEM_LIMIT),
        name="ffn",
    )(x, prew, postw, wg, wu, wd)


def _in_proj_body(x_ref, nw_ref, w_ref, o_ref, h_scr):
    @pl.when(pl.program_id(1) == 0)
    def _():
        def pre(rows):
            h_scr[rows, :] = _rms(x_ref[rows, :], nw_ref[...]).astype(BF16)

        _for_row_chunks(x_ref.shape[0], pre)

    o_ref[...] = _dot(h_scr[...], w_ref[...].astype(BF16))


def _in_proj(x, nw, w, layer):
    L = x.shape[0]
    N = w.shape[2]
    tm = min(2048, L)
    tn = 512
    return pl.pallas_call(
        _in_proj_body,
        grid=(L // tm, N // tn),
        in_specs=[pl.BlockSpec((tm, D_MODEL), lambda i, j: (i, 0), pipeline_mode=pl.Buffered(1)),
                  pl.BlockSpec((1, D_MODEL), lambda i, j: (0, 0)),
                  pl.BlockSpec((None, D_MODEL, tn), lambda i, j: (layer, 0, j))],
        out_specs=pl.BlockSpec((tm, tn), lambda i, j: (i, j)),
        out_shape=jax.ShapeDtypeStruct((L, N), F32),
        scratch_shapes=[pltpu.VMEM((tm, D_MODEL), BF16)],
        compiler_params=_params("parallel", "arbitrary"),
        name="in_proj",
    )(x, nw, w)


def _win_attn_body(nb, q_ref, *refs):
    nkb = WIN_QBLOCKS + 2
    k_refs, v_refs = refs[:nkb], refs[nkb:2 * nkb]
    bias_ref, sink_ref, o_ref = refs[2 * nkb:]
    k_blocks = [r[...].astype(BF16) for r in k_refs]
    v_blocks = [r[...].astype(BF16) for r in v_refs]
    for b in range(WIN_QBLOCKS):
        rows = slice(b * A_BLOCK, (b + 1) * A_BLOCK)
        _win_attn_block(pl.program_id(0) * WIN_QBLOCKS + b, nb, q_ref.at[rows, :],
                        jnp.concatenate(k_blocks[b:b + 3], axis=0),
                        jnp.concatenate(v_blocks[b:b + 3], axis=0),
                        bias_ref, sink_ref, o_ref.at[rows, :])


def _win_attn_block(n, nb, q_ref, k_all, v_all, bias_ref, sink_ref, o_ref):
    edge_lo = jnp.where(n == 0, NEG, 0.0)
    edge_hi = jnp.where(n == nb - 1, NEG, 0.0)
    low = lax.broadcasted_iota(jnp.int32, (A_BLOCK, LANES), 1) < HEAD_DIM
    tile = lambda i: slice(i * LANES, (i + 1) * LANES)
    outs = []
    for g in range(A_KV_HEADS):
        heads = range(g * GROUP, (g + 1) * GROUP)
        in_kv_half = low if g % 2 == 0 else jnp.logical_not(low)
        qs = []
        for h in heads:
            qh = q_ref[:, tile(h // 2)] * (LOG2E * HEAD_DIM ** -0.5)
            if h % 2 != g % 2:
                qh = pltpu.roll(qh, HEAD_DIM, axis=1)
            qs.append(jnp.where(in_kv_half, qh, 0.0).astype(BF16))
        s3 = _dot_nt(jnp.concatenate(qs, axis=0), k_all[:, tile(g // 2)])
        ps, dens = [], []
        for j, h in enumerate(heads):
            s = s3[j * A_BLOCK:(j + 1) * A_BLOCK] + bias_ref[h]
            t = [s[:, 0:LANES] + edge_lo, s[:, LANES:2 * LANES], s[:, 2 * LANES:] + edge_hi]
            sink = sink_ref[h]
            m = jnp.maximum(jnp.max(jnp.maximum(jnp.maximum(t[0], t[1]), t[2]), axis=-1,
                                    keepdims=True), sink)
            p = [jnp.exp2(ti - m) for ti in t]
            dens.append(jnp.sum(p[0] + p[1] + p[2], axis=-1, keepdims=True) + jnp.exp2(sink - m))
            ps.append(jnp.concatenate(p, axis=1).astype(BF16))
        o3 = _dot(jnp.concatenate(ps, axis=0), v_all[:, tile(g // 2)])
        for j, h in enumerate(heads):
            o = o3[j * A_BLOCK:(j + 1) * A_BLOCK] / dens[j]
            outs.append(o if h % 2 == g % 2 else pltpu.roll(o, HEAD_DIM, axis=1))
    for pr in range(A_HEADS // 2):
        o_ref[:, tile(pr)] = jnp.where(low, outs[2 * pr], outs[2 * pr + 1]).astype(o_ref.dtype)


def _win_attn(proj, bias, sink):
    L = proj.shape[0]
    nb = L // A_BLOCK
    qb = WIN_QBLOCKS
    kcol, vcol = A_Q // A_KV, A_Q // A_KV + 1
    kblk = lambda r, c: pl.BlockSpec(
        (A_BLOCK, A_KV), lambda n: (jnp.clip(n * qb - 1 + r, 0, nb - 1), c))
    kv_specs = [kblk(r, c) for c in (kcol, vcol) for r in range(qb + 2)]
    return pl.pallas_call(
        functools.partial(_win_attn_body, nb),
        grid=(nb // qb,),
        in_specs=[pl.BlockSpec((qb * A_BLOCK, A_Q), lambda n: (n, 0))] + kv_specs + [
            pl.BlockSpec((A_HEADS, A_BLOCK, 3 * A_BLOCK), lambda n: (0, 0, 0)),
            pl.BlockSpec((A_HEADS, 1, LANES), lambda n: (0, 0, 0))],
        out_specs=pl.BlockSpec((qb * A_BLOCK, A_Q), lambda n: (n, 0)),
        out_shape=jax.ShapeDtypeStruct((L, A_Q), BF16),
        compiler_params=_params("parallel"),
        name="win_attn",
    )(proj, *([proj] * len(kv_specs)), bias, sink)


def _cumsum_rows(tri, g):
    g1 = g.astype(BF16)
    r1 = g - g1.astype(F32)
    g2 = r1.astype(BF16)
    g3 = (r1 - g2.astype(F32)).astype(BF16)
    return _dot(tri, g1) + _dot(tri, g2) + _dot(tri, g3)


def _hgrn_tables(reverse):
    C = HGRN_CHUNK
    row = lax.broadcasted_iota(jnp.int32, (C, C), 0)
    col = lax.broadcasted_iota(jnp.int32, (C, C), 1)
    tri = ((col >= row) if reverse else (col <= row)).astype(BF16)
    diff = (row ^ col).astype(F32)
    high = (lax.bitcast_convert_type(diff, jnp.int32) >> 23) - 127
    wrong_side = (row < col) if not reverse else (row > col)
    level = jnp.where(row == col, 0, jnp.where(wrong_side, -1, high + 1))
    return tri, level


def _hgrn_chunk(reverse, q, z, v, lb, st_prev, tri, level, later_small):
    C = HGRN_CHUNK
    log_lb = jnp.log(lb)
    log_sig = jnp.minimum(z, 0.0) - jnp.log(1.0 + jnp.exp(-jnp.abs(z)))
    c2 = jnp.log1p(-lb) + log_sig
    g = jnp.maximum(log_lb, c2) + jnp.log(1.0 + jnp.exp(-jnp.abs(log_lb - c2)))
    k = (1.0 - lb) * (1.0 / (1.0 + jnp.exp(z)))

    b = _cumsum_rows(tri, g) * LOG2E
    btot = b[0:1, :] if reverse else b[C - 1:C, :]

    terms = [_dot_nt(q.astype(BF16), k.astype(BF16))]
    seg = b
    for j in range(int(math.log2(C))):
        m = 1 << j
        if m < SUBLANES:
            later = later_small[j] != 0
            up = pltpu.roll(seg, m, axis=0)
            down = pltpu.roll(seg, C - m, axis=0)
            if reverse:
                ref = jnp.where(later, seg, down)
                seg = jnp.where(later, up, seg)
            else:
                ref = jnp.where(later, up, seg)
                seg = jnp.where(later, seg, down)
            x = jnp.exp2(-jnp.abs(b - ref))
        else:
            expo, nxt = [], []
            for p in range(C // (2 * m)):
                lo, hi = slice(2 * p * m, (2 * p + 1) * m), slice((2 * p + 1) * m, (2 * p + 2) * m)
                if reverse:
                    expo += [b[lo] - seg[hi], seg[hi] - b[hi]]
                    nxt += [seg[lo], seg[lo]]
                else:
                    expo += [seg[lo] - b[lo], b[hi] - seg[lo]]
                    nxt += [seg[hi], seg[hi]]
            x = jnp.exp2(jnp.concatenate(expo, axis=0))
            seg = jnp.concatenate(nxt, axis=0)
        terms.append(_dot_nt((q * x).astype(BF16), (k * x).astype(BF16)))
    att = jnp.zeros((C, C), F32)
    for i in reversed(range(len(terms))):
        att = jnp.where(level == i, terms[i], att)
    vb = v.astype(BF16)
    o = _dot(att.astype(BF16), vb)

    o = o + _dot_nt((q * jnp.exp2(b)).astype(BF16), st_prev.astype(BF16))
    kd = (k * jnp.exp2(btot - b)).astype(BF16)
    st_next = jnp.exp2(btot) * st_prev + _dot(v.T.astype(BF16), kd)
    return o, st_next


def _hgrn_body(nch, fwd0_ref, fwd1_ref, bwd0_ref, bwd1_ref, lb_ref, of_ref, ob_ref,
               sf_scr, sb_scr):
    @pl.when(pl.program_id(0) == 0)
    def _():
        sf_scr[...] = jnp.zeros_like(sf_scr)
        sb_scr[...] = jnp.zeros_like(sb_scr)

    C = HGRN_CHUNK
    tables = [_hgrn_tables(False), _hgrn_tables(True)]
    rowv = lax.broadcasted_iota(jnp.int32, (C, LANES), 0)
    later_small = [(rowv >> j) & 1 for j in range(int(math.log2(SUBLANES)))]

    def one(reverse, c, h, halves, z_part, o_ref, s_scr):
        ci = (nch - 1 - c) if reverse else c
        rows = pl.ds(pl.multiple_of(ci * C, C), C)

        def load(part):
            half, off = divmod(part * B_W + h * B_DK, HGRN_IN_WIDTH)
            return halves[half][rows, off:off + B_DK]

        qr = load(0)
        tri, level = tables[int(reverse)]
        o, s_next = _hgrn_chunk(reverse, qr * _sigmoid(qr), load(z_part), load(3),
                                lb_ref[int(reverse), h], s_scr[h], tri, level, later_small)
        s_scr[h] = s_next
        o_ref[rows, h * B_DV:(h + 1) * B_DV] = o

    for c in range(nch):
        for h in range(B_HEADS):
            one(False, c, h, (fwd0_ref, fwd1_ref), 1, of_ref, sf_scr)
            one(True, c, h, (bwd0_ref, bwd1_ref), 2, ob_ref, sb_scr)


def _hgrn(proj, lb2):
    L = proj.shape[0]
    tl = min(512, L)
    nt = L // tl
    nch = tl // HGRN_CHUNK
    lb4 = lb2.reshape(2, B_HEADS, 1, B_DK)
    first = B_COL0 // HGRN_IN_WIDTH
    fwd = lambda k: pl.BlockSpec((tl, HGRN_IN_WIDTH), lambda t: (t, first + k))
    bwd = lambda k: pl.BlockSpec((tl, HGRN_IN_WIDTH), lambda t: (nt - 1 - t, first + k))
    state = pltpu.VMEM((B_HEADS, B_DV, B_DK), F32)
    return pl.pallas_call(
        functools.partial(_hgrn_body, nch),
        grid=(nt,),
        in_specs=[fwd(0), fwd(1), bwd(0), bwd(1),
                  pl.BlockSpec((2, B_HEADS, 1, B_DK), lambda t: (0, 0, 0, 0))],
        out_specs=[pl.BlockSpec((tl, B_VW), lambda t: (t, 0)),
                   pl.BlockSpec((tl, B_VW), lambda t: (nt - 1 - t, 0))],
        out_shape=[jax.ShapeDtypeStruct((L, B_VW), F32), jax.ShapeDtypeStruct((L, B_VW), F32)],
        scratch_shapes=[state, state],
        compiler_params=_params("arbitrary"),
        name="hgrn",
    )(proj, proj, proj, proj, lb4)


def _rope_prep_body(x_ref, qw_ref, kw_ref, cos_ref, sin_ref, q_ref, kt_ref, v_ref):
    cos = cos_ref[...]
    sin = sin_ref[...]
    lane = lax.broadcasted_iota(jnp.int32, cos.shape, 1)
    even = (lane % 2) == 0
    low = lane < HEAD_DIM
    ones_lane = jnp.where(lane == HEAD_DIM, 1.0, 0.0)
    tile = lambda i: slice(i * LANES, (i + 1) * LANES)
    halves = lambda y: (y, pltpu.roll(y, HEAD_DIM, axis=1))

    def norm_rope(x, w):
        sq = x * x
        ms = jnp.where(low, jnp.sum(jnp.where(low, sq, 0.0), axis=-1, keepdims=True),
                       jnp.sum(jnp.where(low, 0.0, sq), axis=-1, keepdims=True)) * (1.0 / HEAD_DIM)
        y = x * lax.rsqrt(ms + EPS) * w
        partner = jnp.where(even, pltpu.roll(y, LANES - 1, axis=1), pltpu.roll(y, 1, axis=1))
        return y * cos + partner * sin

    for t in range(C_Q // LANES):
        y = norm_rope(x_ref[:, tile(t)], qw_ref[...]) * (LOG2E * HEAD_DIM ** -0.5)
        for half, yh in enumerate(halves(y)):
            q_ref[:, tile(2 * t + half)] = yh.astype(q_ref.dtype)
    for t in range(C_KV // LANES):
        y = norm_rope(x_ref[:, tile(C_Q // LANES + t)], kw_ref[...])
        v = x_ref[:, tile((C_Q + C_KV) // LANES + t)]
        for half, (yh, vh) in enumerate(zip(halves(y), halves(v))):
            kt_ref[tile(2 * t + half), :] = jnp.where(low, yh, 0.0).T.astype(kt_ref.dtype)
            v_ref[:, tile(2 * t + half)] = jnp.where(low, vh, ones_lane).astype(v_ref.dtype)


def _rope_prep(proj, qw, kw, cos_t, sin_t):
    L = proj.shape[0]
    tl = min(256, L)
    width = C_Q + 2 * C_KV
    vec = pl.BlockSpec((1, LANES), lambda i: (0, 0))
    tab = pl.BlockSpec((tl, LANES), lambda i: (i, 0))
    return pl.pallas_call(
        _rope_prep_body,
        grid=(L // tl,),
        in_specs=[pl.BlockSpec((tl, width), lambda i: (i, proj.shape[1] // width - 1)),
                  vec, vec, tab, tab],
        out_specs=[pl.BlockSpec((tl, Q_PAD), lambda i: (i, 0)),
                   pl.BlockSpec((KV_PAD, tl), lambda i: (0, i)),
                   pl.BlockSpec((tl, KV_PAD), lambda i: (i, 0))],
        out_shape=[jax.ShapeDtypeStruct((L, Q_PAD), BF16),
                   jax.ShapeDtypeStruct((KV_PAD, L), BF16),
                   jax.ShapeDtypeStruct((L, KV_PAD), BF16)],
        compiler_params=_params("parallel"),
        name="rope_prep",
    )(proj, qw, kw, cos_t, sin_t)


def _axial_attn_body(tq, tk, nk, q_ref, kt_ref, v_ref, o_ref, m_scr, acc_scr):
    q = jnp.concatenate([q_ref[:, h * LANES:(h + 1) * LANES] for h in range(GROUP)], axis=0)
    m_scr[...] = jnp.full(m_scr.shape, -jnp.inf, F32)
    acc_scr[...] = jnp.zeros(acc_scr.shape, F32)

    def step(t, carry):
        k0 = pl.multiple_of(t * tk, tk)
        s = _dot(q, kt_ref[:, pl.ds(k0, tk)])
        parts = [s[:, i * LANES:(i + 1) * LANES] for i in range(tk // LANES)]
        m = m_scr[...]
        m_new = jnp.maximum(m, jnp.max(functools.reduce(jnp.maximum, parts), axis=-1,
                                       keepdims=True))
        p = jnp.concatenate([jnp.exp2(part - m_new) for part in parts], axis=1)
        acc_scr[...] = jnp.exp2(m - m_new) * acc_scr[...] + _dot(p.astype(BF16),
                                                                  v_ref[pl.ds(k0, tk), :])
        m_scr[...] = m_new
        return carry

    lax.fori_loop(0, nk, step, 0, unroll=AXIAL_UNROLL)
    acc = acc_scr[...]
    lane = lax.broadcasted_iota(jnp.int32, acc.shape, 1)
    o = jnp.where(lane < HEAD_DIM, acc / acc[:, HEAD_DIM:HEAD_DIM + 1], 0.0)
    o0, o1, o2 = o[0:tq], o[tq:2 * tq], o[2 * tq:3 * tq]
    o_ref[:, 0:LANES] = (o0 + pltpu.roll(o1, HEAD_DIM, axis=1)).astype(o_ref.dtype)
    o_ref[:, LANES:2 * LANES] = o2.astype(o_ref.dtype)


def _axial_attn(q, kt, v):
    L = q.shape[0]
    tq = min(1024, L)
    tk = min(512, L)
    return pl.pallas_call(
        functools.partial(_axial_attn_body, tq, tk, L // tk),
        grid=(C_KV_HEADS, L // tq),
        in_specs=[pl.BlockSpec((tq, GROUP * LANES), lambda g, i: (i, g)),
                  pl.BlockSpec((LANES, L), lambda g, i: (g, 0)),
                  pl.BlockSpec((L, LANES), lambda g, i: (0, g))],
        out_specs=pl.BlockSpec((tq, 2 * LANES), lambda g, i: (i, g)),
        out_shape=jax.ShapeDtypeStruct((L, C_KV_HEADS * 2 * LANES), BF16),
        scratch_shapes=[pltpu.VMEM((GROUP * tq, LANES), F32), pltpu.VMEM((GROUP * tq, LANES), F32)],
        compiler_params=_params("parallel", "arbitrary"),
        name="axial_attn",
    )(q, kt, v)


def _out_proj_body(x_ref, ya_ref, of_ref, ob_ref, g0_ref, g1_ref, yc_ref, wa_ref, wb_ref, wc_ref,
                   gw_ref, nw_ref, o_ref):
    tot = of_ref[...] + ob_ref[...]
    normed = jnp.concatenate([_rms(tot[:, h * B_DV:(h + 1) * B_DV], gw_ref[...])
                              for h in range(B_HEADS)], axis=1)
    g = jnp.concatenate([g0_ref[...], g1_ref[...]], axis=1)
    yb = (normed * (g * _sigmoid(g))).astype(BF16)
    y = _dot(ya_ref[...], wa_ref[...]) + _dot(yb, wb_ref[...]) + _dot(yc_ref[...], wc_ref[...])
    o_ref[...] = x_ref[...] + _rms(y, nw_ref[...])


def _out_proj(x, ya, o_fwd, o_bwd, proj, yc, wa, wb, wc, gw, nw):
    L = x.shape[0]
    tm = min(512, L)
    row = lambda w: pl.BlockSpec((tm, w), lambda i: (i, 0))
    full = lambda a: pl.BlockSpec(a.shape, lambda i: (0, 0))
    g_col0 = B_COL0 + 3 * B_W + B_VW
    gblk = lambda k: pl.BlockSpec((tm, B_VW // 2), lambda i: (i, g_col0 // (B_VW // 2) + k))
    return pl.pallas_call(
        _out_proj_body,
        grid=(L // tm,),
        in_specs=[row(D_MODEL), row(ya.shape[1]), row(B_VW), row(B_VW), gblk(0), gblk(1),
                  row(yc.shape[1]), full(wa), full(wb), full(wc), full(gw),
                  pl.BlockSpec((1, D_MODEL), lambda i: (0, 0))],
        out_specs=row(D_MODEL),
        out_shape=jax.ShapeDtypeStruct((L, D_MODEL), F32),
        compiler_params=_params("parallel"),
        name="out_proj",
    )(x, ya, o_fwd, o_bwd, proj, proj, yc, wa, wb, wc, gw, nw)


def _split_w_out(w_out):
    wa = w_out[:A_Q]
    wb = w_out[A_Q:A_Q + B_VW]
    wc = w_out[A_Q + B_VW:].reshape(C_KV_HEADS, GROUP * HEAD_DIM, D_MODEL)
    wc = jnp.pad(wc, ((0, 0), (0, 2 * LANES - GROUP * HEAD_DIM), (0, 0)))
    return wa.astype(BF16), wb.astype(BF16), wc.reshape(C_KV_HEADS * 2 * LANES, D_MODEL).astype(BF16)


def _t5_bucket(rel):
    nb = REL_BUCKETS // 2
    max_exact = nb // 2
    n = jnp.abs(rel)
    nf = jnp.maximum(n, 1).astype(F32)
    large = max_exact + (jnp.log(nf / max_exact) / math.log(REL_MAX_DIST / max_exact)
                         * (nb - max_exact)).astype(jnp.int32)
    large = jnp.minimum(large, nb - 1)
    return jnp.where(rel > 0, nb, 0) + jnp.where(n < max_exact, n, large)


def _rope_tables(L):
    half = HEAD_DIM // 2
    inv = 1.0 / (ROPE_THETA ** (jnp.arange(0, half, 2, dtype=F32) / half))
    pos = jnp.arange(L)
    ang = jnp.concatenate([(pos // GRID_W).astype(F32)[:, None] * inv,
                           (pos % GRID_W).astype(F32)[:, None] * inv], axis=-1)
    cos = jnp.repeat(jnp.cos(ang), 2, axis=-1)
    sin = jnp.repeat(jnp.sin(ang), 2, axis=-1) * jnp.tile(jnp.array([-1.0, 1.0], F32), half)
    return jnp.tile(cos, (1, LANES // HEAD_DIM)), jnp.tile(sin, (1, LANES // HEAD_DIM))


def _band_bias(rel_bias):
    period = 4 * A_BLOCK
    rel = jnp.arange(period) - A_BLOCK
    vals = rel_bias.astype(F32)[_t5_bucket(rel)] * LOG2E
    vals = jnp.where((jnp.abs(rel) <= WINDOW)[:, None], vals, NEG).T
    flat = jnp.tile(vals, (1, A_BLOCK))[:, :A_BLOCK * (period - 1)]
    return flat.reshape(A_HEADS, A_BLOCK, period - 1)[:, :, :3 * A_BLOCK]


def kernel(x, w_in, w_out, ffn1_gate, ffn1_up, ffn1_down, ffn2_gate, ffn2_up, ffn2_down,
           norm_w, sink_logits, qk_norm_w, hgrn_lb, hgrn_norm_w, rel_bias):
    B_, L, _ = x.shape
    depth = w_in.shape[0]
    bias = _band_bias(rel_bias)
    cos_t, sin_t = _rope_tables(L)
    lb_c = jnp.cumsum(jax.nn.softmax(hgrn_lb.astype(F32), axis=0), axis=0)
    lbs = lb_c - lb_c[0:1]
    qkw = jnp.tile(qk_norm_w.astype(F32), (1, 1, LANES // HEAD_DIM))
    nw = norm_w.astype(F32).reshape(depth, 6, 1, D_MODEL)
    ffn1_b = (ffn1_gate, ffn1_up, ffn1_down)
    ffn2_b = (ffn2_gate, ffn2_up, ffn2_down)

    outs = []
    for b in range(B_):
        xb = x[b]
        for l in range(depth):
            xb = _ffn(xb, nw[l, 0], nw[l, 1], *ffn1_b, l)
            proj = _in_proj(xb, nw[l, 2], w_in, l)
            ya = _win_attn(proj, bias,
                           jnp.broadcast_to((sink_logits[l].astype(F32) * LOG2E)
                                            .reshape(A_HEADS, 1, 1), (A_HEADS, 1, LANES)))
            o_fwd, o_bwd = _hgrn(proj, lbs[l])
            q, kt, v = _rope_prep(proj, qkw[l, 0:1], qkw[l, 1:2], cos_t, sin_t)
            yc = _axial_attn(q, kt, v)
            oa, ob, oc = _split_w_out(w_out[l])
            xb = _out_proj(xb, ya, o_fwd, o_bwd, proj, yc, oa, ob, oc,
                           hgrn_norm_w[l].astype(F32).reshape(1, B_DV), nw[l, 3])
            xb = _ffn(xb, nw[l, 4], nw[l, 5], *ffn2_b, l)
        outs.append(xb)
    return jnp.stack(outs, axis=0)
```

```python
import functools
import math

import jax
import jax.numpy as jnp
from jax import lax
from jax.experimental import pallas as pl
from jax.experimental.pallas import tpu as pltpu

D_MODEL = 2048
HEAD_DIM = 64
A_HEADS = 12
A_KV_HEADS = 4
WINDOW = 128
A_BLOCK = 128
B_HEADS = 4
B_DK = 128
B_DV = 128
C_HEADS = 12
C_KV_HEADS = 4
ROPE_THETA = 10000.0
GRID_W = 64
REL_BUCKETS = 32
REL_MAX_DIST = 128
D_FF = 5632
EPS = 1e-6
NEG = -1e30

A_Q = A_HEADS * HEAD_DIM
A_KV = A_KV_HEADS * HEAD_DIM
B_W = B_HEADS * B_DK
B_VW = B_HEADS * B_DV
C_Q = C_HEADS * HEAD_DIM
C_KV = C_KV_HEADS * HEAD_DIM
B_COL0 = A_Q + 2 * A_KV

LANES = 128
SUBLANES = 8
ROW_CHUNK = 2 * SUBLANES
Q_PAD = A_HEADS * LANES
KV_PAD = A_KV_HEADS * LANES
GROUP = A_HEADS // A_KV_HEADS
VMEM_LIMIT = 52 * 1024 * 1024
FFN_ROWS = 1024
FFN_COLS = 256
FFN_VMEM_LIMIT = 58 * 1024 * 1024

LOG2E = math.log2(math.e)
AXIAL_UNROLL = 16

WIN_QBLOCKS = 2
HGRN_CHUNK = 128
HGRN_IN_WIDTH = B_COL0

BF16 = jnp.bfloat16
F32 = jnp.float32


def _params(*sem):
    return pltpu.CompilerParams(dimension_semantics=sem, vmem_limit_bytes=VMEM_LIMIT)


def _dot(a, b):
    return jnp.dot(a, b, preferred_element_type=F32)


def _dot_nt(a, b):
    return lax.dot_general(a, b, (((1,), (1,)), ((), ())), preferred_element_type=F32)


def _rms(x, w):
    sq = x * x
    width = x.shape[-1]
    folded = functools.reduce(
        jnp.add, [sq[:, i * LANES:(i + 1) * LANES] for i in range(width // LANES)])
    ms = jnp.sum(folded, axis=-1, keepdims=True) * (1.0 / width)
    return x * lax.rsqrt(ms + EPS) * w


def _for_row_chunks(nrows, fn):
    for start in range(0, nrows, ROW_CHUNK):
        fn(slice(start, start + ROW_CHUNK))


def _sigmoid(x):
    return 1.0 / (1.0 + jnp.exp(-x))


def _ffn_body(emit_next, nf, x_ref, prew_ref, postw_ref, *rest):
    if emit_next:
        nextw_ref, wg_ref, wu_ref, wd_ref, o_ref, hn_ref, h_scr = rest
    else:
        wg_ref, wu_ref, wd_ref, o_ref, h_scr = rest
    j = pl.program_id(1)

    nrows = x_ref.shape[0]

    @pl.when(j == 0)
    def _():
        def pre(rows):
            h_scr[rows, :] = _rms(x_ref[rows, :], prew_ref[...]).astype(BF16)
            o_ref[rows, :] = jnp.zeros((ROW_CHUNK, D_MODEL), F32)

        _for_row_chunks(nrows, pre)

    h = h_scr[...]
    g = _dot(h, wg_ref[...].astype(BF16))
    u = _dot(h, wu_ref[...].astype(BF16))
    a = (g * _sigmoid(g)) * u
    o_ref[...] += _dot(a.astype(BF16), wd_ref[...].astype(BF16))

    @pl.when(j == nf - 1)
    def _():
        def post(rows):
            xn = x_ref[rows, :] + 0.5 * _rms(o_ref[rows, :], postw_ref[...])
            o_ref[rows, :] = xn
            if emit_next:
                hn_ref[rows, :] = _rms(xn, nextw_ref[...]).astype(BF16)

        _for_row_chunks(nrows, post)


def _ffn(x, prew, postw, wg, wu, wd, layer, nextw=None):
    emit_next = nextw is not None
    L = x.shape[0]
    tm = min(FFN_ROWS, L)
    tf = FFN_COLS
    nf = D_FF // tf
    row = pl.BlockSpec((tm, D_MODEL), lambda i, j: (i, 0))
    vec = pl.BlockSpec((1, D_MODEL), lambda i, j: (0, 0))
    norm_args = [prew, postw] + ([nextw] if emit_next else [])
    out_shape = [jax.ShapeDtypeStruct((L, D_MODEL), F32)]
    if emit_next:
        out_shape.append(jax.ShapeDtypeStruct((L, D_MODEL), BF16))
    res = pl.pallas_call(
        functools.partial(_ffn_body, emit_next, nf),
        grid=(L // tm, nf),
        in_specs=[row] + [vec] * len(norm_args) + [
            pl.BlockSpec((None, D_MODEL, tf), lambda i, j: (layer, 0, j)),
            pl.BlockSpec((None, D_MODEL, tf), lambda i, j: (layer, 0, j)),
            pl.BlockSpec((None, tf, D_MODEL), lambda i, j: (layer, j, 0))],
        out_specs=[row] * len(out_shape),
        out_shape=out_shape,
        scratch_shapes=[pltpu.VMEM((tm, D_MODEL), BF16)],
        compiler_params=pltpu.CompilerParams(dimension_semantics=("parallel", "arbitrary"),
                                             vmem_limit_bytes=FFN_VMEM_LIMIT),
        name="ffn",
    )(x, *norm_args, wg, wu, wd)
    return res if emit_next else res[0]


def _in_proj_body(h_ref, w_ref, o_ref):
    o_ref[...] = _dot(h_ref[...], w_ref[...].astype(BF16))


def _in_proj(h, w, layer):
    L = h.shape[0]
    N = w.shape[2]
    tm = min(2048, L)
    tn = 512
    return pl.pallas_call(
        _in_proj_body,
        grid=(L // tm, N // tn),
        in_specs=[pl.BlockSpec((tm, D_MODEL), lambda i, j: (i, 0)),
                  pl.BlockSpec((None, D_MODEL, tn), lambda i, j: (layer, 0, j))],
        out_specs=pl.BlockSpec((tm, tn), lambda i, j: (i, j)),
        out_shape=jax.ShapeDtypeStruct((L, N), F32),
        compiler_params=_params("parallel", "arbitrary"),
        name="in_proj",
    )(h, w)


def _win_attn_body(nb, q_ref, *refs):
    nkb = WIN_QBLOCKS + 2
    k_refs, v_refs = refs[:nkb], refs[nkb:2 * nkb]
    bias_ref, sink_ref, o_ref = refs[2 * nkb:]
    k_blocks = [r[...].astype(BF16) for r in k_refs]
    v_blocks = [r[...].astype(BF16) for r in v_refs]
    for b in range(WIN_QBLOCKS):
        rows = slice(b * A_BLOCK, (b + 1) * A_BLOCK)
        _win_attn_block(pl.program_id(0) * WIN_QBLOCKS + b, nb, q_ref.at[rows, :],
                        jnp.concatenate(k_blocks[b:b + 3], axis=0),
                        jnp.concatenate(v_blocks[b:b + 3], axis=0),
                        bias_ref, sink_ref, o_ref.at[rows, :])


def _win_attn_block(n, nb, q_ref, k_all, v_all, bias_ref, sink_ref, o_ref):
    edge_lo = jnp.where(n == 0, NEG, 0.0)
    edge_hi = jnp.where(n == nb - 1, NEG, 0.0)
    low = lax.broadcasted_iota(jnp.int32, (A_BLOCK, LANES), 1) < HEAD_DIM
    tile = lambda i: slice(i * LANES, (i + 1) * LANES)
    outs = []
    for g in range(A_KV_HEADS):
        heads = range(g * GROUP, (g + 1) * GROUP)
        in_kv_half = low if g % 2 == 0 else jnp.logical_not(low)
        qs = []
        for h in heads:
            qh = q_ref[:, tile(h // 2)] * (LOG2E * HEAD_DIM ** -0.5)
            if h % 2 != g % 2:
                qh = pltpu.roll(qh, HEAD_DIM, axis=1)
            qs.append(jnp.where(in_kv_half, qh, 0.0).astype(BF16))
        s3 = _dot_nt(jnp.concatenate(qs, axis=0), k_all[:, tile(g // 2)])
        ps, dens = [], []
        for j, h in enumerate(heads):
            s = s3[j * A_BLOCK:(j + 1) * A_BLOCK] + bias_ref[h]
            t = [s[:, 0:LANES] + edge_lo, s[:, LANES:2 * LANES], s[:, 2 * LANES:] + edge_hi]
            sink = sink_ref[h]
            m = jnp.maximum(jnp.max(jnp.maximum(jnp.maximum(t[0], t[1]), t[2]), axis=-1,
                                    keepdims=True), sink)
            p = [jnp.exp2(ti - m) for ti in t]
            dens.append(jnp.sum(p[0] + p[1] + p[2], axis=-1, keepdims=True) + jnp.exp2(sink - m))
            ps.append(jnp.concatenate(p, axis=1).astype(BF16))
        o3 = _dot(jnp.concatenate(ps, axis=0), v_all[:, tile(g // 2)])
        for j, h in enumerate(heads):
            o = o3[j * A_BLOCK:(j + 1) * A_BLOCK] / dens[j]
            outs.append(o if h % 2 == g % 2 else pltpu.roll(o, HEAD_DIM, axis=1))
    for pr in range(A_HEADS // 2):
        o_ref[:, tile(pr)] = jnp.where(low, outs[2 * pr], outs[2 * pr + 1]).astype(o_ref.dtype)


def _win_attn(proj, bias, sink):
    L = proj.shape[0]
    nb = L // A_BLOCK
    qb = WIN_QBLOCKS
    kcol, vcol = A_Q // A_KV, A_Q // A_KV + 1
    kblk = lambda r, c: pl.BlockSpec(
        (A_BLOCK, A_KV), lambda n: (jnp.clip(n * qb - 1 + r, 0, nb - 1), c))
    kv_specs = [kblk(r, c) for c in (kcol, vcol) for r in range(qb + 2)]
    return pl.pallas_call(
        functools.partial(_win_attn_body, nb),
        grid=(nb // qb,),
        in_specs=[pl.BlockSpec((qb * A_BLOCK, A_Q), lambda n: (n, 0))] + kv_specs + [
            pl.BlockSpec((A_HEADS, A_BLOCK, 3 * A_BLOCK), lambda n: (0, 0, 0)),
            pl.BlockSpec((A_HEADS, 1, LANES), lambda n: (0, 0, 0))],
        out_specs=pl.BlockSpec((qb * A_BLOCK, A_Q), lambda n: (n, 0)),
        out_shape=jax.ShapeDtypeStruct((L, A_Q), BF16),
        compiler_params=_params("parallel"),
        name="win_attn",
    )(proj, *([proj] * len(kv_specs)), bias, sink)


def _cumsum_rows(tri, g):
    g1 = g.astype(BF16)
    r1 = g - g1.astype(F32)
    g2 = r1.astype(BF16)
    g3 = (r1 - g2.astype(F32)).astype(BF16)
    return _dot(tri, g1) + _dot(tri, g2) + _dot(tri, g3)


def _hgrn_tables(reverse):
    C = HGRN_CHUNK
    row = lax.broadcasted_iota(jnp.int32, (C, C), 0)
    col = lax.broadcasted_iota(jnp.int32, (C, C), 1)
    tri = ((col >= row) if reverse else (col <= row)).astype(BF16)
    diff = (row ^ col).astype(F32)
    high = (lax.bitcast_convert_type(diff, jnp.int32) >> 23) - 127
    wrong_side = (row < col) if not reverse else (row > col)
    level = jnp.where(row == col, 0, jnp.where(wrong_side, -1, high + 1))
    return tri, level


def _hgrn_chunk(reverse, q, z, v, lb, st_prev, tri, level, later_small):
    C = HGRN_CHUNK
    log_lb = jnp.log(lb)
    log_sig = jnp.minimum(z, 0.0) - jnp.log(1.0 + jnp.exp(-jnp.abs(z)))
    c2 = jnp.log1p(-lb) + log_sig
    g = jnp.maximum(log_lb, c2) + jnp.log(1.0 + jnp.exp(-jnp.abs(log_lb - c2)))
    k = (1.0 - lb) * (1.0 / (1.0 + jnp.exp(z)))

    b = _cumsum_rows(tri, g) * LOG2E
    btot = b[0:1, :] if reverse else b[C - 1:C, :]

    terms = [_dot_nt(q.astype(BF16), k.astype(BF16))]
    seg = b
    for j in range(int(math.log2(C))):
        m = 1 << j
        if m < SUBLANES:
            later = later_small[j] != 0
            up = pltpu.roll(seg, m, axis=0)
            down = pltpu.roll(seg, C - m, axis=0)
            if reverse:
                ref = jnp.where(later, seg, down)
                seg = jnp.where(later, up, seg)
            else:
                ref = jnp.where(later, up, seg)
                seg = jnp.where(later, seg, down)
            x = jnp.exp2(-jnp.abs(b - ref))
        else:
            expo, nxt = [], []
            for p in range(C // (2 * m)):
                lo, hi = slice(2 * p * m, (2 * p + 1) * m), slice((2 * p + 1) * m, (2 * p + 2) * m)
                if reverse:
                    expo += [b[lo] - seg[hi], seg[hi] - b[hi]]
                    nxt += [seg[lo], seg[lo]]
                else:
                    expo += [seg[lo] - b[lo], b[hi] - seg[lo]]
                    nxt += [seg[hi], seg[hi]]
            x = jnp.exp2(jnp.concatenate(expo, axis=0))
            seg = jnp.concatenate(nxt, axis=0)
        terms.append(_dot_nt((q * x).astype(BF16), (k * x).astype(BF16)))
    att = jnp.zeros((C, C), F32)
    for i in reversed(range(len(terms))):
        att = jnp.where(level == i, terms[i], att)
    vb = v.astype(BF16)
    o = _dot(att.astype(BF16), vb)

    o = o + _dot_nt((q * jnp.exp2(b)).astype(BF16), st_prev.astype(BF16))
    kd = (k * jnp.exp2(btot - b)).astype(BF16)
    st_next = jnp.exp2(btot) * st_prev + _dot(v.T.astype(BF16), kd)
    return o, st_next


def _hgrn_body(nch, fwd0_ref, fwd1_ref, bwd0_ref, bwd1_ref, lb_ref, of_ref, ob_ref,
               sf_scr, sb_scr):
    @pl.when(pl.program_id(0) == 0)
    def _():
        sf_scr[...] = jnp.zeros_like(sf_scr)
        sb_scr[...] = jnp.zeros_like(sb_scr)

    C = HGRN_CHUNK
    tables = [_hgrn_tables(False), _hgrn_tables(True)]
    rowv = lax.broadcasted_iota(jnp.int32, (C, LANES), 0)
    later_small = [(rowv >> j) & 1 for j in range(int(math.log2(SUBLANES)))]

    def one(reverse, c, h, halves, z_part, o_ref, s_scr):
        ci = (nch - 1 - c) if reverse else c
        rows = pl.ds(pl.multiple_of(ci * C, C), C)

        def load(part):
            half, off = divmod(part * B_W + h * B_DK, HGRN_IN_WIDTH)
            return halves[half][rows, off:off + B_DK]

        qr = load(0)
        tri, level = tables[int(reverse)]
        o, s_next = _hgrn_chunk(reverse, qr * _sigmoid(qr), load(z_part), load(3),
                                lb_ref[int(reverse), h], s_scr[h], tri, level, later_small)
        s_scr[h] = s_next
        o_ref[rows, h * B_DV:(h + 1) * B_DV] = o

    for c in range(nch):
        for h in range(B_HEADS):
            one(False, c, h, (fwd0_ref, fwd1_ref), 1, of_ref, sf_scr)
            one(True, c, h, (bwd0_ref, bwd1_ref), 2, ob_ref, sb_scr)


def _hgrn(proj, lb2):
    L = proj.shape[0]
    tl = min(512, L)
    nt = L // tl
    nch = tl // HGRN_CHUNK
    lb4 = lb2.reshape(2, B_HEADS, 1, B_DK)
    first = B_COL0 // HGRN_IN_WIDTH
    fwd = lambda k: pl.BlockSpec((tl, HGRN_IN_WIDTH), lambda t: (t, first + k))
    bwd = lambda k: pl.BlockSpec((tl, HGRN_IN_WIDTH), lambda t: (nt - 1 - t, first + k))
    state = pltpu.VMEM((B_HEADS, B_DV, B_DK), F32)
    return pl.pallas_call(
        functools.partial(_hgrn_body, nch),
        grid=(nt,),
        in_specs=[fwd(0), fwd(1), bwd(0), bwd(1),
                  pl.BlockSpec((2, B_HEADS, 1, B_DK), lambda t: (0, 0, 0, 0))],
        out_specs=[pl.BlockSpec((tl, B_VW), lambda t: (t, 0)),
                   pl.BlockSpec((tl, B_VW), lambda t: (nt - 1 - t, 0))],
        out_shape=[jax.ShapeDtypeStruct((L, B_VW), F32), jax.ShapeDtypeStruct((L, B_VW), F32)],
        scratch_shapes=[state, state],
        compiler_params=_params("arbitrary"),
        name="hgrn",
    )(proj, proj, proj, proj, lb4)


def _rope_prep_body(x_ref, qw_ref, kw_ref, cos_ref, sin_ref, q_ref, kt_ref, v_ref):
    cos = cos_ref[...]
    sin = sin_ref[...]
    lane = lax.broadcasted_iota(jnp.int32, cos.shape, 1)
    even = (lane % 2) == 0
    low = lane < HEAD_DIM
    ones_lane = jnp.where(lane == HEAD_DIM, 1.0, 0.0)
    tile = lambda i: slice(i * LANES, (i + 1) * LANES)
    halves = lambda y: (y, pltpu.roll(y, HEAD_DIM, axis=1))

    def norm_rope(x, w):
        sq = x * x
        ms = jnp.where(low, jnp.sum(jnp.where(low, sq, 0.0), axis=-1, keepdims=True),
                       jnp.sum(jnp.where(low, 0.0, sq), axis=-1, keepdims=True)) * (1.0 / HEAD_DIM)
        y = x * lax.rsqrt(ms + EPS) * w
        partner = jnp.where(even, pltpu.roll(y, LANES - 1, axis=1), pltpu.roll(y, 1, axis=1))
        return y * cos + partner * sin

    for t in range(C_Q // LANES):
        y = norm_rope(x_ref[:, tile(t)], qw_ref[...]) * (LOG2E * HEAD_DIM ** -0.5)
        for half, yh in enumerate(halves(y)):
            q_ref[:, tile(2 * t + half)] = yh.astype(q_ref.dtype)
    for t in range(C_KV // LANES):
        y = norm_rope(x_ref[:, tile(C_Q // LANES + t)], kw_ref[...])
        v = x_ref[:, tile((C_Q + C_KV) // LANES + t)]
        for half, (yh, vh) in enumerate(zip(halves(y), halves(v))):
            kt_ref[tile(2 * t + half), :] = jnp.where(low, yh, 0.0).T.astype(kt_ref.dtype)
            v_ref[:, tile(2 * t + half)] = jnp.where(low, vh, ones_lane).astype(v_ref.dtype)


def _rope_prep(proj, qw, kw, cos_t, sin_t):
    L = proj.shape[0]
    tl = min(256, L)
    width = C_Q + 2 * C_KV
    vec = pl.BlockSpec((1, LANES), lambda i: (0, 0))
    tab = pl.BlockSpec((tl, LANES), lambda i: (i, 0))
    return pl.pallas_call(
        _rope_prep_body,
        grid=(L // tl,),
        in_specs=[pl.BlockSpec((tl, width), lambda i: (i, proj.shape[1] // width - 1)),
                  vec, vec, tab, tab],
        out_specs=[pl.BlockSpec((tl, Q_PAD), lambda i: (i, 0)),
                   pl.BlockSpec((KV_PAD, tl), lambda i: (0, i)),
                   pl.BlockSpec((tl, KV_PAD), lambda i: (i, 0))],
        out_shape=[jax.ShapeDtypeStruct((L, Q_PAD), BF16),
                   jax.ShapeDtypeStruct((KV_PAD, L), BF16),
                   jax.ShapeDtypeStruct((L, KV_PAD), BF16)],
        compiler_params=_params("parallel"),
        name="rope_prep",
    )(proj, qw, kw, cos_t, sin_t)


def _axial_attn_body(tq, tk, nk, q_ref, kt_ref, v_ref, o_ref, m_scr, acc_scr):
    q = jnp.concatenate([q_ref[:, h * LANES:(h + 1) * LANES] for h in range(GROUP)], axis=0)
    m_scr[...] = jnp.full(m_scr.shape, -jnp.inf, F32)
    acc_scr[...] = jnp.zeros(acc_scr.shape, F32)

    def step(t, carry):
        k0 = pl.multiple_of(t * tk, tk)
        s = _dot(q, kt_ref[:, pl.ds(k0, tk)])
        parts = [s[:, i * LANES:(i + 1) * LANES] for i in range(tk // LANES)]
        m = m_scr[...]
        m_new = jnp.maximum(m, jnp.max(functools.reduce(jnp.maximum, parts), axis=-1,
                                       keepdims=True))
        p = jnp.concatenate([jnp.exp2(part - m_new) for part in parts], axis=1)
        acc_scr[...] = jnp.exp2(m - m_new) * acc_scr[...] + _dot(p.astype(BF16),
                                                                  v_ref[pl.ds(k0, tk), :])
        m_scr[...] = m_new
        return carry

    lax.fori_loop(0, nk, step, 0, unroll=AXIAL_UNROLL)
    acc = acc_scr[...]
    lane = lax.broadcasted_iota(jnp.int32, acc.shape, 1)
    o = jnp.where(lane < HEAD_DIM, acc / acc[:, HEAD_DIM:HEAD_DIM + 1], 0.0)
    o0, o1, o2 = o[0:tq], o[tq:2 * tq], o[2 * tq:3 * tq]
    o_ref[:, 0:LANES] = (o0 + pltpu.roll(o1, HEAD_DIM, axis=1)).astype(o_ref.dtype)
    o_ref[:, LANES:2 * LANES] = o2.astype(o_ref.dtype)


def _axial_attn(q, kt, v):
    L = q.shape[0]
    tq = min(1024, L)
    tk = min(512, L)
    return pl.pallas_call(
        functools.partial(_axial_attn_body, tq, tk, L // tk),
        grid=(C_KV_HEADS, L // tq),
        in_specs=[pl.BlockSpec((tq, GROUP * LANES), lambda g, i: (i, g)),
                  pl.BlockSpec((LANES, L), lambda g, i: (g, 0)),
                  pl.BlockSpec((L, LANES), lambda g, i: (0, g))],
        out_specs=pl.BlockSpec((tq, 2 * LANES), lambda g, i: (i, g)),
        out_shape=jax.ShapeDtypeStruct((L, C_KV_HEADS * 2 * LANES), BF16),
        scratch_shapes=[pltpu.VMEM((GROUP * tq, LANES), F32), pltpu.VMEM((GROUP * tq, LANES), F32)],
        compiler_params=_params("parallel", "arbitrary"),
        name="axial_attn",
    )(q, kt, v)


def _out_proj_body(x_ref, ya_ref, of_ref, ob_ref, g0_ref, g1_ref, yc_ref, wa_ref, wb_ref, wc_ref,
                   gw_ref, nw_ref, o_ref):
    tot = of_ref[...] + ob_ref[...]
    normed = jnp.concatenate([_rms(tot[:, h * B_DV:(h + 1) * B_DV], gw_ref[...])
                              for h in range(B_HEADS)], axis=1)
    g = jnp.concatenate([g0_ref[...], g1_ref[...]], axis=1)
    yb = (normed * (g * _sigmoid(g))).astype(BF16)
    y = _dot(ya_ref[...], wa_ref[...]) + _dot(yb, wb_ref[...]) + _dot(yc_ref[...], wc_ref[...])
    o_ref[...] = x_ref[...] + _rms(y, nw_ref[...])


def _out_proj(x, ya, o_fwd, o_bwd, proj, yc, wa, wb, wc, gw, nw):
    L = x.shape[0]
    tm = min(512, L)
    row = lambda w: pl.BlockSpec((tm, w), lambda i: (i, 0))
    full = lambda a: pl.BlockSpec(a.shape, lambda i: (0, 0))
    g_col0 = B_COL0 + 3 * B_W + B_VW
    gblk = lambda k: pl.BlockSpec((tm, B_VW // 2), lambda i: (i, g_col0 // (B_VW // 2) + k))
    return pl.pallas_call(
        _out_proj_body,
        grid=(L // tm,),
        in_specs=[row(D_MODEL), row(ya.shape[1]), row(B_VW), row(B_VW), gblk(0), gblk(1),
                  row(yc.shape[1]), full(wa), full(wb), full(wc), full(gw),
                  pl.BlockSpec((1, D_MODEL), lambda i: (0, 0))],
        out_specs=row(D_MODEL),
        out_shape=jax.ShapeDtypeStruct((L, D_MODEL), F32),
        compiler_params=_params("parallel"),
        name="out_proj",
    )(x, ya, o_fwd, o_bwd, proj, proj, yc, wa, wb, wc, gw, nw)


def _split_w_out(w_out):
    wa = w_out[:A_Q]
    wb = w_out[A_Q:A_Q + B_VW]
    wc = w_out[A_Q + B_VW:].reshape(C_KV_HEADS, GROUP * HEAD_DIM, D_MODEL)
    wc = jnp.pad(wc, ((0, 0), (0, 2 * LANES - GROUP * HEAD_DIM), (0, 0)))
    return wa.astype(BF16), wb.astype(BF16), wc.reshape(C_KV_HEADS * 2 * LANES, D_MODEL).astype(BF16)


def _t5_bucket(rel):
    nb = REL_BUCKETS // 2
    max_exact = nb // 2
    n = jnp.abs(rel)
    nf = jnp.maximum(n, 1).astype(F32)
    large = max_exact + (jnp.log(nf / max_exact) / math.log(REL_MAX_DIST / max_exact)
                         * (nb - max_exact)).astype(jnp.int32)
    large = jnp.minimum(large, nb - 1)
    return jnp.where(rel > 0, nb, 0) + jnp.where(n < max_exact, n, large)


def _rope_tables(L):
    half = HEAD_DIM // 2
    inv = 1.0 / (ROPE_THETA ** (jnp.arange(0, half, 2, dtype=F32) / half))
    pos = jnp.arange(L)
    ang = jnp.concatenate([(pos // GRID_W).astype(F32)[:, None] * inv,
                           (pos % GRID_W).astype(F32)[:, None] * inv], axis=-1)
    cos = jnp.repeat(jnp.cos(ang), 2, axis=-1)
    sin = jnp.repeat(jnp.sin(ang), 2, axis=-1) * jnp.tile(jnp.array([-1.0, 1.0], F32), half)
    return jnp.tile(cos, (1, LANES // HEAD_DIM)), jnp.tile(sin, (1, LANES // HEAD_DIM))


def _band_bias(rel_bias):
    period = 4 * A_BLOCK
    rel = jnp.arange(period) - A_BLOCK
    vals = rel_bias.astype(F32)[_t5_bucket(rel)] * LOG2E
    vals = jnp.where((jnp.abs(rel) <= WINDOW)[:, None], vals, NEG).T
    flat = jnp.tile(vals, (1, A_BLOCK))[:, :A_BLOCK * (period - 1)]
    return flat.reshape(A_HEADS, A_BLOCK, period - 1)[:, :, :3 * A_BLOCK]


def kernel(x, w_in, w_out, ffn1_gate, ffn1_up, ffn1_down, ffn2_gate, ffn2_up, ffn2_down,
           norm_w, sink_logits, qk_norm_w, hgrn_lb, hgrn_norm_w, rel_bias):
    B_, L, _ = x.shape
    depth = w_in.shape[0]
    bias = _band_bias(rel_bias)
    cos_t, sin_t = _rope_tables(L)
    lb_c = jnp.cumsum(jax.nn.softmax(hgrn_lb.astype(F32), axis=0), axis=0)
    lbs = lb_c - lb_c[0:1]
    qkw = jnp.tile(qk_norm_w.astype(F32), (1, 1, LANES // HEAD_DIM))
    nw = norm_w.astype(F32).reshape(depth, 6, 1, D_MODEL)
    ffn1_b = (ffn1_gate, ffn1_up, ffn1_down)
    ffn2_b = (ffn2_gate, ffn2_up, ffn2_down)

    outs = []
    for b in range(B_):
        xb = x[b]
        for l in range(depth):
            xb, h = _ffn(xb, nw[l, 0], nw[l, 1], *ffn1_b, l, nextw=nw[l, 2])
            proj = _in_proj(h, w_in, l)
            ya = _win_attn(proj, bias,
                           jnp.broadcast_to((sink_logits[l].astype(F32) * LOG2E)
                                            .reshape(A_HEADS, 1, 1), (A_HEADS, 1, LANES)))
            o_fwd, o_bwd = _hgrn(proj, lbs[l])
            q, kt, v = _rope_prep(proj, qkw[l, 0:1], qkw[l, 1:2], cos_t, sin_t)
            yc = _axial_attn(q, kt, v)
            oa, ob, oc = _split_w_out(w_out[l])
            xb = _out_proj(xb, ya, o_fwd, o_bwd, proj, yc, oa, ob, oc,
                           hgrn_norm_w[l].astype(F32).reshape(1, B_DV), nw[l, 3])
            xb = _ffn(xb, nw[l, 4], nw[l, 5], *ffn2_b, l)
        outs.append(xb)
    return jnp.stack(outs, axis=0)
```
